```python
import jax, jax.numpy as jnp
from jax import lax
import numpy as np

D_MODEL = 2048
BATCH = 4
SEQ = 8192
DEPTH = 1
DEC_BATCH = 32
DEC_SEQ = 64
PAST_LEN = 2048

CHUNK = 64
Q_BLOCK = 128
N_HEADS = 8
QK_NOPE = 128
QK_ROPE = 64
V_HEAD = 128
Q_RANK = 512
KV_RANK = 256
MLA_WIDTH = N_HEADS * V_HEAD
POOL_WINDOWS = (2, 4, 8, 16)
N_POOL_GROUPS = len(POOL_WINDOWS)
POOL_WIDTH = D_MODEL - MLA_WIDTH
POOL_GROUP = POOL_WIDTH // N_POOL_GROUPS
POOL_HIST = max(POOL_WINDOWS) - 1
IN_WIDTH = Q_RANK + KV_RANK + QK_ROPE + POOL_WIDTH
D_FF = 5632
PLE_DIM = 256
ROPE_THETA = 10000.0
LN_EPS = 1e-5
RMS_EPS = 1e-6
ATTN_SCALE = (QK_NOPE + QK_ROPE) ** -0.5
DEEPNORM_ALPHA = (2.0 * DEPTH) ** 0.25
DEEPNORM_BETA = (8.0 * DEPTH) ** -0.25

kernel_name = "hybrid_mla_pool_streaming_encoder_step"


def layer_norm(x, g, b):
    xf = x.astype(jnp.float32)
    mu = jnp.mean(xf, -1, keepdims=True)
    var = jnp.mean(jnp.square(xf - mu), -1, keepdims=True)
    return ((xf - mu) * lax.rsqrt(var + LN_EPS) * g + b).astype(x.dtype)


def rms_norm(x, g):
    xf = x.astype(jnp.float32)
    return (xf * lax.rsqrt(jnp.mean(jnp.square(xf), -1, keepdims=True) + RMS_EPS) * g).astype(x.dtype)


def swiglu(x, w_gu, w_down):
    gate, up = jnp.split(x @ w_gu, 2, axis=-1)
    return (jax.nn.silu(gate) * up) @ w_down


def rope_tables(pos):
    inv = 1.0 / (ROPE_THETA ** (jnp.arange(0, QK_ROPE, 2, dtype=jnp.float32) / QK_ROPE))
    ang = pos.astype(jnp.float32)[:, None] * inv[None, :]
    ang = jnp.concatenate([ang, ang], -1)
    return jnp.cos(ang), jnp.sin(ang)


def apply_rope(x, cos, sin):
    x1, x2 = jnp.split(x, 2, axis=-1)
    rot = jnp.concatenate([-x2, x1], -1)
    return (x * cos + rot * sin).astype(x.dtype)


def latent_attention(q_lat, q_rope, c_kv, k_rope, mask):
    s = (jnp.einsum('bqhc,bkc->bhqk', q_lat, c_kv, preferred_element_type=jnp.float32)
         + jnp.einsum('bqhr,bkr->bhqk', q_rope, k_rope, preferred_element_type=jnp.float32)) * ATTN_SCALE
    if mask is not None:
        s = jnp.where(mask, s, -jnp.inf)
    pr = jax.nn.softmax(s, axis=-1).astype(c_kv.dtype)
    return jnp.einsum('bhqk,bkc->bqhc', pr, c_kv)


def prompt_attention(q_lat, q_rope, c_kv, k_rope):
    B, S = q_lat.shape[:2]
    nb = S // Q_BLOCK
    key_chunk = jnp.arange(S) // CHUNK

    def block(args):
        ql, qr, i = args
        q_chunk = (i * Q_BLOCK + jnp.arange(Q_BLOCK)) // CHUNK
        mask = key_chunk[None, :] <= q_chunk[:, None]
        return latent_attention(ql, qr, c_kv, k_rope, mask)

    ql = q_lat.reshape(B, nb, Q_BLOCK, N_HEADS, KV_RANK).swapaxes(0, 1)
    qr = q_rope.reshape(B, nb, Q_BLOCK, N_HEADS, QK_ROPE).swapaxes(0, 1)
    o = lax.map(block, (ql, qr, jnp.arange(nb)))
    return o.swapaxes(0, 1).reshape(B, S, N_HEADS, KV_RANK)


def pool_mixer(u, hist, offset, w_pool, pool_scale):
    B, L = u.shape[:2]
    full = jnp.concatenate([hist, u], 1)
    upf = full.astype(jnp.float32)
    cs = jnp.cumsum(upf, axis=1)
    cs = jnp.concatenate([jnp.zeros_like(cs[:, :1]), cs], 1)
    end = cs[:, POOL_HIST + 1:]
    pos = offset + jnp.arange(L)
    means = []
    for g, w in enumerate(POOL_WINDOWS):
        sl = slice(g * POOL_GROUP, (g + 1) * POOL_GROUP)
        start = cs[:, POOL_HIST + 1 - w: POOL_HIST + 1 - w + L, sl]
        cnt = jnp.minimum(w, pos + 1).astype(jnp.float32)[None, :, None]
        means.append((end[..., sl] - start) / cnt)
    d = (jnp.concatenate(means, -1) - upf[:, POOL_HIST:]).astype(u.dtype)
    d = d.reshape(B, L, N_POOL_GROUPS, POOL_GROUP)
    y = jnp.einsum('blgc,gcd->blgd', d, w_pool).reshape(B, L, POOL_WIDTH) * pool_scale
    return y, full[:, -POOL_HIST:]


def encoder_layer(x, p, past_c, past_kr, pool_hist, offset, w):
    B, L, _ = x.shape
    x = layer_norm(DEEPNORM_ALPHA * x + 0.5 * swiglu(x, w['ffn1_gu'], w['ffn1_down']), w['ln1_g'], w['ln1_b'])
    z = x @ w['w_in']
    c_q = rms_norm(z[..., :Q_RANK], w['g_q'])
    c_kv = rms_norm(z[..., Q_RANK:Q_RANK + KV_RANK], w['g_kv'])
    k_rope = z[..., Q_RANK + KV_RANK:Q_RANK + KV_RANK + QK_ROPE]
    u = z[..., Q_RANK + KV_RANK + QK_ROPE:]
    cos, sin = rope_tables(offset + jnp.arange(L))
    q_nope = jnp.einsum('blc,chd->blhd', c_q, w['w_uq_nope'])
    q_rope = apply_rope(jnp.einsum('blc,chr->blhr', c_q, w['w_uq_rope']), cos[:, None, :], sin[:, None, :])
    k_rope = apply_rope(k_rope, cos, sin)
    q_lat = jnp.einsum('blhd,chd->blhc', q_nope, w['w_uk'])
    if past_c is None:
        o_lat = prompt_attention(q_lat, q_rope, c_kv, k_rope)
    else:
        o_lat = latent_attention(q_lat, q_rope,
                                 jnp.concatenate([past_c, c_kv], 1),
                                 jnp.concatenate([past_kr, k_rope], 1), None)
    o_attn = jnp.einsum('blhc,chd->blhd', o_lat, w['w_uv']).reshape(B, L, MLA_WIDTH)
    o_pool, new_hist = pool_mixer(u, pool_hist, offset, w['w_pool'], w['pool_scale'])
    mix = jnp.concatenate([o_attn, o_pool], -1) @ w['w_o']
    x = layer_norm(DEEPNORM_ALPHA * x + mix, w['ln2_g'], w['ln2_b'])
    x = layer_norm(DEEPNORM_ALPHA * x + 0.5 * swiglu(x, w['ffn2_gu'], w['ffn2_down']), w['ln3_g'], w['ln3_b'])
    ple = jax.nn.sigmoid(x @ w['w_ple_gate']) * (p @ w['w_ple_proj'])
    x = layer_norm(DEEPNORM_ALPHA * x + ple, w['ln4_g'], w['ln4_b'])
    return x, c_kv, k_rope, new_hist


def setup_inputs(seed: int = 0) -> dict:
    key = jax.random.key(seed)
    ks = iter(jax.random.split(key, 48))
    f = jnp.float32

    def nrm(shape, scale=1.0):
        return jax.random.normal(next(ks), shape, f) * scale

    def gain(n):
        return 1.0 + nrm((DEPTH, n), 0.02)

    return {
        "x_prompt": nrm((BATCH, SEQ, D_MODEL)),
        "x_sample": nrm((DEC_BATCH, DEC_SEQ, D_MODEL)),
        "cache_kv_latent": nrm((DEPTH, DEC_BATCH, PAST_LEN, KV_RANK)),
        "cache_k_rope": nrm((DEPTH, DEC_BATCH, PAST_LEN, QK_ROPE)),
        "state_pool": nrm((DEPTH, DEC_BATCH, POOL_HIST, POOL_WIDTH)),
        "p_prompt": nrm((DEPTH, BATCH, SEQ, PLE_DIM)),
        "p_sample": nrm((DEPTH, DEC_BATCH, DEC_SEQ, PLE_DIM)),
        "ffn1_gu": nrm((DEPTH, D_MODEL, 2 * D_FF), D_MODEL ** -0.5),
        "ffn1_down": nrm((DEPTH, D_FF, D_MODEL), DEEPNORM_BETA * D_FF ** -0.5),
        "ln1_g": gain(D_MODEL),
        "ln1_b": nrm((DEPTH, D_MODEL), 0.02),
        "w_in": nrm((DEPTH, D_MODEL, IN_WIDTH), D_MODEL ** -0.5),
        "g_q": gain(Q_RANK),
        "g_kv": gain(KV_RANK),
        "w_uq_nope": nrm((DEPTH, Q_RANK, N_HEADS, QK_NOPE), Q_RANK ** -0.5),
        "w_uq_rope": nrm((DEPTH, Q_RANK, N_HEADS, QK_ROPE), Q_RANK ** -0.5),
        "w_uk": nrm((DEPTH, KV_RANK, N_HEADS, QK_NOPE), KV_RANK ** -0.5),
        "w_uv": nrm((DEPTH, KV_RANK, N_HEADS, V_HEAD), KV_RANK ** -0.5),
        "w_pool": nrm((DEPTH, N_POOL_GROUPS, POOL_GROUP, POOL_GROUP), POOL_GROUP ** -0.5),
        "pool_scale": 1.0 + nrm((DEPTH, POOL_WIDTH), 0.1),
        "w_o": nrm((DEPTH, D_MODEL, D_MODEL), DEEPNORM_BETA * D_MODEL ** -0.5),
        "ln2_g": gain(D_MODEL),
        "ln2_b": nrm((DEPTH, D_MODEL), 0.02),
        "ffn2_gu": nrm((DEPTH, D_MODEL, 2 * D_FF), D_MODEL ** -0.5),
        "ffn2_down": nrm((DEPTH, D_FF, D_MODEL), DEEPNORM_BETA * D_FF ** -0.5),
        "ln3_g": gain(D_MODEL),
        "ln3_b": nrm((DEPTH, D_MODEL), 0.02),
        "w_ple_gate": nrm((DEPTH, D_MODEL, D_MODEL), D_MODEL ** -0.5),
        "w_ple_proj": nrm((DEPTH, PLE_DIM, D_MODEL), DEEPNORM_BETA * PLE_DIM ** -0.5),
        "ln4_g": gain(D_MODEL),
        "ln4_b": nrm((DEPTH, D_MODEL), 0.02),
    }


def reference(x_prompt, x_sample, cache_kv_latent, cache_k_rope, state_pool, p_prompt, p_sample,
              ffn1_gu, ffn1_down, ln1_g, ln1_b, w_in, g_q, g_kv, w_uq_nope, w_uq_rope, w_uk, w_uv,
              w_pool, pool_scale, w_o, ln2_g, ln2_b, ffn2_gu, ffn2_down, ln3_g, ln3_b,
              w_ple_gate, w_ple_proj, ln4_g, ln4_b):
    hp, hs = x_prompt, x_sample
    ckv_p, kr_p, pool_p, ckv_s, kr_s, pool_s = [], [], [], [], [], []
    for i in range(DEPTH):
        w = {
            'ffn1_gu': ffn1_gu[i], 'ffn1_down': ffn1_down[i], 'ln1_g': ln1_g[i], 'ln1_b': ln1_b[i],
            'w_in': w_in[i], 'g_q': g_q[i], 'g_kv': g_kv[i],
            'w_uq_nope': w_uq_nope[i], 'w_uq_rope': w_uq_rope[i], 'w_uk': w_uk[i], 'w_uv': w_uv[i],
            'w_pool': w_pool[i], 'pool_scale': pool_scale[i], 'w_o': w_o[i],
            'ln2_g': ln2_g[i], 'ln2_b': ln2_b[i],
            'ffn2_gu': ffn2_gu[i], 'ffn2_down': ffn2_down[i], 'ln3_g': ln3_g[i], 'ln3_b': ln3_b[i],
            'w_ple_gate': w_ple_gate[i], 'w_ple_proj': w_ple_proj[i], 'ln4_g': ln4_g[i], 'ln4_b': ln4_b[i],
        }
        zero_hist = jnp.zeros((hp.shape[0], POOL_HIST, POOL_WIDTH), hp.dtype)
        hp, c1, k1, s1 = encoder_layer(hp, p_prompt[i], None, None, zero_hist, 0, w)
        hs, c2, k2, s2 = encoder_layer(hs, p_sample[i], cache_kv_latent[i], cache_k_rope[i],
                                       state_pool[i], PAST_LEN, w)
        ckv_p.append(c1); kr_p.append(k1); pool_p.append(s1)
        ckv_s.append(c2); kr_s.append(k2); pool_s.append(s2)
    new_kv_latent_prompt = jnp.stack(ckv_p)
    new_k_rope_prompt = jnp.stack(kr_p)
    new_pool_prompt = jnp.stack(pool_p)
    new_kv_latent_sample = jnp.stack(ckv_s)
    new_k_rope_sample = jnp.stack(kr_s)
    new_pool_sample = jnp.stack(pool_s)
    return (hp, hs, new_kv_latent_prompt, new_k_rope_prompt, new_pool_prompt,
            new_kv_latent_sample, new_k_rope_sample, new_pool_sample)
```

```python
import functools
import math

import jax
import jax.numpy as jnp
from jax import lax
from jax.experimental import pallas as pl
from jax.experimental.pallas import tpu as pltpu

CHUNK = 64
N_HEADS = 8
QK_NOPE = 128
QK_ROPE = 64
V_HEAD = 128
POOL_WINDOWS = (2, 4, 8, 16)
POOL_HIST = max(POOL_WINDOWS) - 1
HALO = POOL_HIST + 1
ROPE_THETA = 10000.0
LN_EPS = 1e-5
RMS_EPS = 1e-6
ATTN_SCALE = (QK_NOPE + QK_ROPE) ** -0.5
SCORE_SCALE = ATTN_SCALE * math.log2(math.e)

V7X_VMEM_LIMIT_BYTES = 56 * 1024 * 1024

_F32 = jnp.float32
_BF16 = jnp.bfloat16
_NT_DIMS = (((1,), (1,)), ((), ()))


def _dot(a, b):
    return jnp.dot(a, b, preferred_element_type=_F32)


def _dot_nt(a, b):
    return lax.dot_general(a, b, _NT_DIMS, preferred_element_type=_F32)


def _layer_norm(y, g, b):
    mu = jnp.mean(y, axis=-1, keepdims=True)
    d = y - mu
    var = jnp.mean(d * d, axis=-1, keepdims=True)
    return d * lax.rsqrt(var + LN_EPS) * g + b


def _rms_norm(z, g):
    return z * lax.rsqrt(jnp.mean(z * z, axis=-1, keepdims=True) + RMS_EPS) * g


def _params(semantics):
    return pltpu.CompilerParams(
        dimension_semantics=semantics, vmem_limit_bytes=V7X_VMEM_LIMIT_BYTES)


def _ffn_ln_kernel(x_ref, wg_ref, wu_ref, wd_ref, g_ref, b_ref, o_ref, xb_ref, *, alpha):
    j = pl.program_id(1)

    @pl.when(j == 0)
    def _():
        xb_ref[...] = x_ref[...].astype(_BF16)
        o_ref[...] = jnp.zeros_like(o_ref)

    xb = xb_ref[...]
    gate = _dot(xb, wg_ref[...])
    up = _dot(xb, wu_ref[...])
    act = (gate * jax.nn.sigmoid(gate) * up).astype(_BF16)
    o_ref[...] += _dot(act, wd_ref[...])

    @pl.when(j == pl.num_programs(1) - 1)
    def _():
        y = alpha * x_ref[...] + 0.5 * o_ref[...]
        o_ref[...] = _layer_norm(y, g_ref[...], b_ref[...])


def _ffn_ln(x, w_gu, w_down, g, b, *, alpha, tm, tf):
    t, d = x.shape
    f = w_down.shape[0]
    nj = f // tf
    return pl.pallas_call(
        functools.partial(_ffn_ln_kernel, alpha=alpha),
        grid=(t // tm, nj),
        in_specs=[
            pl.BlockSpec((tm, d), lambda i, j: (i, 0)),
            pl.BlockSpec((d, tf), lambda i, j: (0, j)),
            pl.BlockSpec((d, tf), lambda i, j: (0, j + nj)),
            pl.BlockSpec((tf, d), lambda i, j: (j, 0)),
            pl.BlockSpec((1, d), lambda i, j: (0, 0)),
            pl.BlockSpec((1, d), lambda i, j: (0, 0)),
        ],
        out_specs=pl.BlockSpec((tm, d), lambda i, j: (i, 0)),
        out_shape=jax.ShapeDtypeStruct((t, d), _F32),
        scratch_shapes=[pltpu.VMEM((tm, d), _BF16)],
        compiler_params=_params(("parallel", "arbitrary")),
        name="ffn_ln",
    )(x, w_gu, w_gu, w_down, g, b)


def _inproj_kernel(x_ref, win_ref, gq_ref, gkv_ref, wq_ref, wukt_ref, cs_ref, cc_ref, ss_ref,
                   qlat_ref, qrope_ref, ckv_ref, kr_ref, ckvb_ref, krb_ref, u_ref,
                   *, q_rank, kv_rank):
    xb = x_ref[...].astype(_BF16)
    z = _dot(xb, win_ref[...])
    o_kv = q_rank
    o_kr = o_kv + kv_rank
    o_u = o_kr + 2 * QK_ROPE
    u_ref[...] = z[:, o_u:]

    c_kv = _rms_norm(z[:, o_kv:o_kr], gkv_ref[...])
    ckv_ref[...] = c_kv
    ckvb_ref[...] = c_kv.astype(_BF16)

    kt = z[:, o_kr:o_u] * cs_ref[...]
    k_rope = kt[:, :QK_ROPE] + kt[:, QK_ROPE:]
    kr_ref[...] = k_rope
    krb_ref[...] = k_rope.astype(_BF16)

    c_q = _rms_norm(z[:, :q_rank], gq_ref[...]).astype(_BF16)
    qz = _dot(c_q, wq_ref[...])
    n_nope = N_HEADS * QK_NOPE
    n_rope = N_HEADS * QK_ROPE
    lane_reps = n_rope // cc_ref.shape[1]
    cos = jnp.concatenate([cc_ref[...]] * lane_reps, axis=1)
    sin = jnp.concatenate([ss_ref[...]] * lane_reps, axis=1)
    q_rope = (qz[:, n_nope:n_nope + n_rope] * cos + qz[:, n_nope + n_rope:] * sin) * SCORE_SCALE
    q_rope = q_rope.astype(_BF16)
    q_nope = qz[:, :n_nope].astype(_BF16)
    for h in range(N_HEADS):
        q_lat = _dot(q_nope[:, h * QK_NOPE:(h + 1) * QK_NOPE], wukt_ref[h]) * SCORE_SCALE
        qlat_ref[h] = q_lat.astype(_BF16)
        qrope_ref[h] = q_rope[:, h * QK_ROPE:(h + 1) * QK_ROPE]


def _inproj(x, w_in, g_q, g_kv, w_q, w_ukt, cs, cc, ss, *, tm):
    t, d = x.shape
    q_rank = g_q.shape[1]
    kv_rank = g_kv.shape[1]
    pool_width = w_in.shape[1] - q_rank - kv_rank - 2 * QK_ROPE
    nper = cs.shape[0] // tm
    const2 = lambda i: (0, 0)
    row = lambda i: (i, 0)
    tab = lambda i: (i % nper, 0)
    head_row = lambda i: (0, i, 0)
    return pl.pallas_call(
        functools.partial(_inproj_kernel, q_rank=q_rank, kv_rank=kv_rank),
        grid=(t // tm,),
        in_specs=[
            pl.BlockSpec((tm, d), row),
            pl.BlockSpec(w_in.shape, const2),
            pl.BlockSpec((1, q_rank), const2),
            pl.BlockSpec((1, kv_rank), const2),
            pl.BlockSpec(w_q.shape, const2),
            pl.BlockSpec(w_ukt.shape, lambda i: (0, 0, 0)),
            pl.BlockSpec((tm, cs.shape[1]), tab),
            pl.BlockSpec((tm, cc.shape[1]), tab),
            pl.BlockSpec((tm, ss.shape[1]), tab),
        ],
        out_specs=[
            pl.BlockSpec((N_HEADS, tm, kv_rank), head_row),
            pl.BlockSpec((N_HEADS, tm, QK_ROPE), head_row),
            pl.BlockSpec((tm, kv_rank), row),
            pl.BlockSpec((tm, QK_ROPE), row),
            pl.BlockSpec((tm, kv_rank), row),
            pl.BlockSpec((tm, QK_ROPE), row),
            pl.BlockSpec((tm, pool_width), row),
        ],
        out_shape=[
            jax.ShapeDtypeStruct((N_HEADS, t, kv_rank), _BF16),
            jax.ShapeDtypeStruct((N_HEADS, t, QK_ROPE), _BF16),
            jax.ShapeDtypeStruct((t, kv_rank), _F32),
            jax.ShapeDtypeStruct((t, QK_ROPE), _F32),
            jax.ShapeDtypeStruct((t, kv_rank), _BF16),
            jax.ShapeDtypeStruct((t, QK_ROPE), _BF16),
            jax.ShapeDtypeStruct((t, pool_width), _F32),
        ],
        compiler_params=_params(("parallel",)),
        name="inproj",
    )(x, w_in, g_q, g_kv, w_q, w_ukt, cs, cc, ss)


def _attn_prompt_kernel(ql_ref, qr_ref, kv_ref, kr_ref, o_ref, m_ref, l_ref, acc_ref, *, tq, tk):
    i = pl.program_id(1)
    rows = N_HEADS * tq
    ql = ql_ref[...].reshape(rows, ql_ref.shape[2])
    qr = qr_ref[...].reshape(rows, qr_ref.shape[2])
    m_ref[...] = jnp.full_like(m_ref, -jnp.inf)
    l_ref[...] = jnp.zeros_like(l_ref)
    acc_ref[...] = jnp.zeros_like(acc_ref)

    def tile(j, masked):
        start = pl.multiple_of(j * tk, tk)
        kv = kv_ref[pl.ds(start, tk), :]
        kr = kr_ref[pl.ds(start, tk), :]
        s = _dot_nt(ql, kv) + _dot_nt(qr, kr)
        if masked:
            q_tok = lax.broadcasted_iota(jnp.int32, (rows, 1), 0) % tq + i * tq
            k_tok = lax.broadcasted_iota(jnp.int32, (1, tk), 1) + start
            s = jnp.where(k_tok // CHUNK <= q_tok // CHUNK, s, -jnp.inf)
        m_old = m_ref[...]
        m_new = jnp.maximum(m_old, jnp.max(s, axis=1, keepdims=True))
        p = jnp.exp2(s - m_new)
        a = jnp.exp2(m_old - m_new)
        l_ref[...] = a * l_ref[...] + jnp.sum(p, axis=1, keepdims=True)
        acc_ref[...] = a * acc_ref[...] + _dot(p.astype(_BF16), kv)
        m_ref[...] = m_new

    n_full = (i * tq) // tk

    def body(j, carry):
        tile(j, masked=False)
        return carry

    lax.fori_loop(0, n_full, body, 0)
    tile(n_full, masked=True)
    o = acc_ref[...] / l_ref[...]
    o_ref[...] = o.reshape(o_ref.shape).astype(o_ref.dtype)


def _attn_prompt(qlat, qrope, ckvb, krb, *, batch, seq, tq, tk):
    assert tk % tq == 0 and tq % CHUNK == 0 and seq % tk == 0
    h, t, c = qlat.shape
    r = qrope.shape[2]
    nq = seq // tq
    rows = h * tq
    head_row = lambda b, i: (0, b * nq + i, 0)
    return pl.pallas_call(
        functools.partial(_attn_prompt_kernel, tq=tq, tk=tk),
        grid=(batch, nq),
        in_specs=[
            pl.BlockSpec((h, tq, c), head_row),
            pl.BlockSpec((h, tq, r), head_row),
            pl.BlockSpec((seq, c), lambda b, i: (b, 0)),
            pl.BlockSpec((seq, r), lambda b, i: (b, 0)),
        ],
        out_specs=pl.BlockSpec((h, tq, c), head_row),
        out_shape=jax.ShapeDtypeStruct((h, t, c), _BF16),
        scratch_shapes=[
            pltpu.VMEM((rows, 1), _F32),
            pltpu.VMEM((rows, 1), _F32),
            pltpu.VMEM((rows, c), _F32),
        ],
        compiler_params=_params(("parallel", "arbitrary")),
        name="attn_prompt",
    )(qlat, qrope, ckvb, krb)


def _attn_sample_kernel(ql_ref, qr_ref, pkv_ref, pkr_ref, kv_ref, kr_ref, o_ref):
    rows = ql_ref.shape[0] * ql_ref.shape[1]
    ql = ql_ref[...].reshape(rows, ql_ref.shape[2])
    qr = qr_ref[...].reshape(rows, qr_ref.shape[2])
    pkv = pkv_ref[0].astype(_BF16)
    pkr = pkr_ref[0].astype(_BF16)
    kv = kv_ref[...]
    kr = kr_ref[...]
    s_past = _dot_nt(ql, pkv) + _dot_nt(qr, pkr)
    s_new = _dot_nt(ql, kv) + _dot_nt(qr, kr)
    m = jnp.maximum(jnp.max(s_past, axis=1, keepdims=True), jnp.max(s_new, axis=1, keepdims=True))
    p_past = jnp.exp2(s_past - m)
    p_new = jnp.exp2(s_new - m)
    l = jnp.sum(p_past, axis=1, keepdims=True) + jnp.sum(p_new, axis=1, keepdims=True)
    o = (_dot(p_past.astype(_BF16), pkv) + _dot(p_new.astype(_BF16), kv)) / l
    o_ref[...] = o.reshape(o_ref.shape).astype(o_ref.dtype)


def _attn_sample(qlat, qrope, past_kv, past_kr, ckvb, krb, *, seq):
    h, t, c = qlat.shape
    r = qrope.shape[2]
    batch, past_len, _ = past_kv.shape
    head_row = lambda b: (0, b, 0)
    return pl.pallas_call(
        _attn_sample_kernel,
        grid=(batch,),
        in_specs=[
            pl.BlockSpec((h, seq, c), head_row),
            pl.BlockSpec((h, seq, r), head_row),
            pl.BlockSpec((1, past_len, c), lambda b: (b, 0, 0)),
            pl.BlockSpec((1, past_len, r), lambda b: (b, 0, 0)),
            pl.BlockSpec((seq, c), lambda b: (b, 0)),
            pl.BlockSpec((seq, r), lambda b: (b, 0)),
        ],
        out_specs=pl.BlockSpec((h, seq, c), head_row),
        out_shape=jax.ShapeDtypeStruct((h, t, c), _BF16),
        compiler_params=_params(("parallel",)),
        name="attn_sample",
    )(qlat, qrope, past_kv, past_kr, ckvb, krb)


def _pool_delta(ext, pos0, group):
    n = ext.shape[0] - HALO
    pos = pos0 + lax.broadcasted_iota(jnp.int32, (n, 1), 0)
    out = []
    for gi, w in enumerate(POOL_WINDOWS):
        cols = ext[:, gi * group:(gi + 1) * group]
        s = cols
        shift = 1
        while shift < w:
            s = s + pltpu.roll(s, shift, axis=0)
            shift *= 2
        cnt = jnp.minimum(w, pos + 1).astype(_F32)
        out.append(s[HALO:] / cnt - cols[HALO:])
    return out


def _outproj_kernel(olat_ref, u_ref, halo_ref, x_ref, wuv_ref, wpool_ref, pscale_ref, wo_ref,
                    g_ref, b_ref, o_ref, *, alpha, seq, offset, blocks_per_seq):
    tm = u_ref.shape[0]
    group = u_ref.shape[1] // len(POOL_WINDOWS)
    u = u_ref[...]
    halo = halo_ref[...]
    if blocks_per_seq is not None:
        bi = pl.program_id(0) % blocks_per_seq
        halo = jnp.where(bi == 0, 0.0, halo)
        deltas = [_pool_delta(jnp.concatenate([halo, u], axis=0), offset + bi * tm, group)]
    else:
        deltas = []
        for s in range(tm // seq):
            ext = jnp.concatenate(
                [halo[s * HALO:(s + 1) * HALO], u[s * seq:(s + 1) * seq]], axis=0)
            deltas.append(_pool_delta(ext, offset, group))
    pooled = []
    for gi in range(len(POOL_WINDOWS)):
        d = jnp.concatenate([ds[gi] for ds in deltas], axis=0) if len(deltas) > 1 else deltas[0][gi]
        pooled.append(_dot(d.astype(_BF16), wpool_ref[gi]))
    o_pool = jnp.concatenate(pooled, axis=1) * pscale_ref[...]
    o_attn = jnp.concatenate(
        [_dot(olat_ref[h], wuv_ref[h]) for h in range(N_HEADS)], axis=1)
    mixed = jnp.concatenate([o_attn, o_pool], axis=1).astype(_BF16)
    y = alpha * x_ref[...] + _dot(mixed, wo_ref[...])
    o_ref[...] = _layer_norm(y, g_ref[...], b_ref[...])


def _outproj(olat, u, halo_src, x, w_uv, w_pool, pool_scale, w_o, g, b,
             *, alpha, seq, offset, tm, prompt):
    t, d = x.shape
    h, _, c = olat.shape
    pw = u.shape[1]
    row = lambda i: (i, 0)
    const2 = lambda i: (0, 0)
    const3 = lambda i: (0, 0, 0)
    if prompt:
        blocks_per_seq = seq // tm
        step = tm // HALO
        halo_spec = pl.BlockSpec((HALO, pw), lambda i: (jnp.maximum(i * step - 1, 0), 0))
    else:
        blocks_per_seq = None
        halo_spec = pl.BlockSpec(((tm // seq) * HALO, pw), row)
    return pl.pallas_call(
        functools.partial(_outproj_kernel, alpha=alpha, seq=seq, offset=offset,
                          blocks_per_seq=blocks_per_seq),
        grid=(t // tm,),
        in_specs=[
            pl.BlockSpec((h, tm, c), lambda i: (0, i, 0)),
            pl.BlockSpec((tm, pw), row),
            halo_spec,
            pl.BlockSpec((tm, d), row),
            pl.BlockSpec(w_uv.shape, const3),
            pl.BlockSpec(w_pool.shape, const3),
            pl.BlockSpec((1, pw), const2),
            pl.BlockSpec(w_o.shape, const2),
            pl.BlockSpec((1, d), const2),
            pl.BlockSpec((1, d), const2),
        ],
        out_specs=pl.BlockSpec((tm, d), row),
        out_shape=jax.ShapeDtypeStruct((t, d), _F32),
        compiler_params=_params(("parallel",)),
        name="outproj_prompt" if prompt else "outproj_sample",
    )(olat, u, halo_src, x, w_uv, w_pool, pool_scale, w_o, g, b)


def _ple_kernel(x_ref, p_ref, wg_ref, wp_ref, g_ref, b_ref, o_ref, *, alpha):
    x = x_ref[...]
    gate = jax.nn.sigmoid(_dot(x.astype(_BF16), wg_ref[...]))
    proj = _dot(p_ref[...].astype(_BF16), wp_ref[...])
    o_ref[...] = _layer_norm(alpha * x + gate * proj, g_ref[...], b_ref[...])


def _ple(x, p, w_gate, w_proj, g, b, *, alpha, tm):
    t, d = x.shape
    pd = p.shape[1]
    row = lambda i: (i, 0)
    const2 = lambda i: (0, 0)
    return pl.pallas_call(
        functools.partial(_ple_kernel, alpha=alpha),
        grid=(t // tm,),
        in_specs=[
            pl.BlockSpec((tm, d), row),
            pl.BlockSpec((tm, pd), row),
            pl.BlockSpec(w_gate.shape, const2),
            pl.BlockSpec(w_proj.shape, const2),
            pl.BlockSpec((1, d), const2),
            pl.BlockSpec((1, d), const2),
        ],
        out_specs=pl.BlockSpec((tm, d), row),
        out_shape=jax.ShapeDtypeStruct((t, d), _F32),
        compiler_params=_params(("parallel",)),
        name="ple",
    )(x, p, w_gate, w_proj, g, b)


def _rotate_half_cols(w):
    half = w.shape[-1] // 2
    return jnp.concatenate([-w[..., half:], w[..., :half]], axis=-1)


def _rope_tables(pos, reps):
    inv = 1.0 / (ROPE_THETA ** (jnp.arange(0, QK_ROPE, 2, dtype=_F32) / QK_ROPE))
    ang = pos.astype(_F32)[:, None] * inv[None, :]
    ang = jnp.concatenate([ang, ang], -1)
    cos, sin = jnp.cos(ang), jnp.sin(ang)
    tile = lambda a: jnp.tile(a, (reps, 1))
    return (tile(jnp.concatenate([cos, sin], -1)), tile(jnp.concatenate([cos, cos], -1)),
            tile(jnp.concatenate([sin, sin], -1)))


def _tile_rows(total, preferred):
    tile = min(total, preferred)
    assert total % tile == 0
    return tile


def _prep_layer_weights(w):
    q_rank = w['g_q'].shape[0]
    kv_rank = w['g_kv'].shape[0]
    w_in = w['w_in']
    o_kr = q_rank + kv_rank
    w_kr = w_in[:, o_kr:o_kr + QK_ROPE]
    w_in_all = jnp.concatenate(
        [w_in[:, :o_kr], w_kr, _rotate_half_cols(w_kr), w_in[:, o_kr + QK_ROPE:]], axis=1)
    w_q = jnp.concatenate(
        [w['w_uq_nope'].reshape(q_rank, -1), w['w_uq_rope'].reshape(q_rank, -1),
         _rotate_half_cols(w['w_uq_rope']).reshape(q_rank, -1)], axis=1)
    row = lambda v: v.reshape(1, -1)
    return dict(
        ffn1_gu=w['ffn1_gu'].astype(_BF16), ffn1_down=w['ffn1_down'].astype(_BF16),
        ffn2_gu=w['ffn2_gu'].astype(_BF16), ffn2_down=w['ffn2_down'].astype(_BF16),
        w_in=w_in_all.astype(_BF16), w_q=w_q.astype(_BF16),
        w_ukt=jnp.transpose(w['w_uk'], (1, 2, 0)).astype(_BF16),
        w_uv=jnp.transpose(w['w_uv'], (1, 0, 2)).astype(_BF16),
        w_pool=w['w_pool'].astype(_BF16), pool_scale=row(w['pool_scale']),
        w_o=w['w_o'].astype(_BF16),
        w_ple_gate=w['w_ple_gate'].astype(_BF16), w_ple_proj=w['w_ple_proj'].astype(_BF16),
        g_q=row(w['g_q']), g_kv=row(w['g_kv']),
        ln1=(row(w['ln1_g']), row(w['ln1_b'])), ln2=(row(w['ln2_g']), row(w['ln2_b'])),
        ln3=(row(w['ln3_g']), row(w['ln3_b'])), ln4=(row(w['ln4_g']), row(w['ln4_b'])),
    )


def _encoder_layer(x, p, past, pool_hist, offset, pw, *, alpha):
    bsz, seq, d = x.shape
    t = bsz * seq
    prompt = past is None
    xf = x.reshape(t, d)
    tm = _tile_rows(t, 512)
    d_ff = pw['ffn1_down'].shape[0]
    tf = 512 if d_ff % 512 == 0 else d_ff

    x1 = _ffn_ln(xf, pw['ffn1_gu'], pw['ffn1_down'], *pw['ln1'], alpha=alpha, tm=tm, tf=tf)

    tmi = _tile_rows(t, 256)
    if prompt:
        assert seq % tmi == 0
        tabs = _rope_tables(offset + jnp.arange(seq), 1)
    else:
        assert tmi % seq == 0
        tabs = _rope_tables(offset + jnp.arange(seq), tmi // seq)
    qlat, qrope, ckv, kr, ckvb, krb, u = _inproj(
        x1, pw['w_in'], pw['g_q'], pw['g_kv'], pw['w_q'], pw['w_ukt'], *tabs, tm=tmi)

    if prompt:
        tq = min(seq, 128)
        tk = min(seq, 512)
        olat = _attn_prompt(qlat, qrope, ckvb, krb, batch=bsz, seq=seq, tq=tq, tk=tk)
        halo_src = u
    else:
        olat = _attn_sample(qlat, qrope, past[0], past[1], ckvb, krb, seq=seq)
        halo_src = jnp.pad(pool_hist, ((0, 0), (HALO - POOL_HIST, 0), (0, 0))).reshape(bsz * HALO, -1)

    tmo = _tile_rows(t, 512 if prompt else 256)
    if prompt:
        assert seq % tmo == 0 and tmo % HALO == 0
    else:
        assert tmo % seq == 0
    x2 = _outproj(olat, u, halo_src, x1, pw['w_uv'], pw['w_pool'], pw['pool_scale'], pw['w_o'],
                  *pw['ln2'], alpha=alpha, seq=seq, offset=offset, tm=tmo, prompt=prompt)
    x3 = _ffn_ln(x2, pw['ffn2_gu'], pw['ffn2_down'], *pw['ln3'], alpha=alpha, tm=tm, tf=tf)
    y = _ple(x3, p.reshape(t, -1), pw['w_ple_gate'], pw['w_ple_proj'], *pw['ln4'],
             alpha=alpha, tm=tm)

    u3 = u.reshape(bsz, seq, -1)
    if seq >= POOL_HIST:
        new_hist = u3[:, seq - POOL_HIST:]
    else:
        new_hist = jnp.concatenate([pool_hist, u3], axis=1)[:, -POOL_HIST:]
    return (y.reshape(bsz, seq, d), ckv.reshape(bsz, seq, -1), kr.reshape(bsz, seq, -1), new_hist)


def kernel(x_prompt, x_sample, cache_kv_latent, cache_k_rope, state_pool, p_prompt, p_sample, ffn1_gu, ffn1_down, ln1_g, ln1_b, w_in, g_q, g_kv, w_uq_nope, w_uq_rope, w_uk, w_uv, w_pool, pool_scale, w_o, ln2_g, ln2_b, ffn2_gu, ffn2_down, ln3_g, ln3_b, w_ple_gate, w_ple_proj, ln4_g, ln4_b):
    depth = ffn1_gu.shape[0]
    past_len = cache_kv_latent.shape[2]
    alpha = (2.0 * depth) ** 0.25
    stacked = dict(
        ffn1_gu=ffn1_gu, ffn1_down=ffn1_down, ln1_g=ln1_g, ln1_b=ln1_b, w_in=w_in, g_q=g_q,
        g_kv=g_kv, w_uq_nope=w_uq_nope, w_uq_rope=w_uq_rope, w_uk=w_uk, w_uv=w_uv,
        w_pool=w_pool, pool_scale=pool_scale, w_o=w_o, ln2_g=ln2_g, ln2_b=ln2_b,
        ffn2_gu=ffn2_gu, ffn2_down=ffn2_down, ln3_g=ln3_g, ln3_b=ln3_b,
        w_ple_gate=w_ple_gate, w_ple_proj=w_ple_proj, ln4_g=ln4_g, ln4_b=ln4_b)
    hp, hs = x_prompt, x_sample
    outs = [[] for _ in range(6)]
    for i in range(depth):
        pw = _prep_layer_weights({k: v[i] for k, v in stacked.items()})
        hp, c1, k1, s1 = _encoder_layer(hp, p_prompt[i], None, None, 0, pw, alpha=alpha)
        hs, c2, k2, s2 = _encoder_layer(
            hs, p_sample[i], (cache_kv_latent[i], cache_k_rope[i]), state_pool[i], past_len, pw,
            alpha=alpha)
        for acc, v in zip(outs, (c1, k1, s1, c2, k2, s2)):
            acc.append(v)
    return (hp, hs) + tuple(jnp.stack(v) for v in outs)
```

```python
import functools
import math

import jax
import jax.numpy as jnp
from jax import lax
from jax.experimental import pallas as pl
from jax.experimental.pallas import tpu as pltpu

CHUNK = 64
N_HEADS = 8
QK_NOPE = 128
QK_ROPE = 64
V_HEAD = 128
POOL_WINDOWS = (2, 4, 8, 16)
POOL_HIST = max(POOL_WINDOWS) - 1
HALO = POOL_HIST + 1
ROPE_THETA = 10000.0
LN_EPS = 1e-5
RMS_EPS = 1e-6
ATTN_SCALE = (QK_NOPE + QK_ROPE) ** -0.5
SCORE_SCALE = ATTN_SCALE * math.log2(math.e)

V7X_VMEM_LIMIT_BYTES = 56 * 1024 * 1024

_F32 = jnp.float32
_BF16 = jnp.bfloat16
_NT_DIMS = (((1,), (1,)), ((), ()))


def _dot(a, b):
    return jnp.dot(a, b, preferred_element_type=_F32)


def _dot_nt(a, b):
    return lax.dot_general(a, b, _NT_DIMS, preferred_element_type=_F32)


def _layer_norm(y, g, b):
    mu = jnp.mean(y, axis=-1, keepdims=True)
    d = y - mu
    var = jnp.mean(d * d, axis=-1, keepdims=True)
    return d * lax.rsqrt(var + LN_EPS) * g + b


def _rms_norm(z, g):
    return z * lax.rsqrt(jnp.mean(z * z, axis=-1, keepdims=True) + RMS_EPS) * g


def _params(semantics):
    return pltpu.CompilerParams(
        dimension_semantics=semantics, vmem_limit_bytes=V7X_VMEM_LIMIT_BYTES)


def _ffn_ln_kernel(x_ref, wg_ref, wu_ref, wd_ref, g_ref, b_ref, o_ref, xb_ref, *, alpha):
    j = pl.program_id(1)

    @pl.when(j == 0)
    def _():
        xb_ref[...] = x_ref[...].astype(_BF16)
        o_ref[...] = jnp.zeros_like(o_ref)

    xb = xb_ref[...]
    gate = _dot(xb, wg_ref[...])
    up = _dot(xb, wu_ref[...])
    act = (gate * jax.nn.sigmoid(gate) * up).astype(_BF16)
    o_ref[...] += _dot(act, wd_ref[...])

    @pl.when(j == pl.num_programs(1) - 1)
    def _():
        y = alpha * x_ref[...] + 0.5 * o_ref[...]
        o_ref[...] = _layer_norm(y, g_ref[...], b_ref[...])


def _ffn_ln(x, w_gu, w_down, g, b, *, alpha, tm, tf):
    t, d = x.shape
    f = w_down.shape[0]
    nj = f // tf
    return pl.pallas_call(
        functools.partial(_ffn_ln_kernel, alpha=alpha),
        grid=(t // tm, nj),
        in_specs=[
            pl.BlockSpec((tm, d), lambda i, j: (i, 0)),
            pl.BlockSpec((d, tf), lambda i, j: (0, j)),
            pl.BlockSpec((d, tf), lambda i, j: (0, j + nj)),
            pl.BlockSpec((tf, d), lambda i, j: (j, 0)),
            pl.BlockSpec((1, d), lambda i, j: (0, 0)),
            pl.BlockSpec((1, d), lambda i, j: (0, 0)),
        ],
        out_specs=pl.BlockSpec((tm, d), lambda i, j: (i, 0)),
        out_shape=jax.ShapeDtypeStruct((t, d), _F32),
        scratch_shapes=[pltpu.VMEM((tm, d), _BF16)],
        compiler_params=_params(("parallel", "arbitrary")),
        name="ffn_ln",
    )(x, w_gu, w_gu, w_down, g, b)


def _inproj_kernel(x_ref, win_ref, gq_ref, gkv_ref, wq_ref, wukt_ref, cs_ref, cc_ref, ss_ref,
                   qlat_ref, qrope_ref, ckv_ref, kr_ref, ckvb_ref, krb_ref, u_ref,
                   *, q_rank, kv_rank):
    xb = x_ref[...].astype(_BF16)
    z = _dot(xb, win_ref[...])
    o_kv = q_rank
    o_kr = o_kv + kv_rank
    o_u = o_kr + 2 * QK_ROPE
    u_ref[...] = z[:, o_u:]

    c_kv = _rms_norm(z[:, o_kv:o_kr], gkv_ref[...])
    ckv_ref[...] = c_kv
    ckvb_ref[...] = c_kv.astype(_BF16)

    kt = z[:, o_kr:o_u] * cs_ref[...]
    k_rope = kt[:, :QK_ROPE] + kt[:, QK_ROPE:]
    kr_ref[...] = k_rope
    krb_ref[...] = k_rope.astype(_BF16)

    c_q = _rms_norm(z[:, :q_rank], gq_ref[...]).astype(_BF16)
    qz = _dot(c_q, wq_ref[...])
    n_nope = N_HEADS * QK_NOPE
    n_rope = N_HEADS * QK_ROPE
    lane_reps = n_rope // cc_ref.shape[1]
    cos = jnp.concatenate([cc_ref[...]] * lane_reps, axis=1)
    sin = jnp.concatenate([ss_ref[...]] * lane_reps, axis=1)
    q_rope = (qz[:, n_nope:n_nope + n_rope] * cos + qz[:, n_nope + n_rope:] * sin) * SCORE_SCALE
    q_rope = q_rope.astype(_BF16)
    q_nope = qz[:, :n_nope].astype(_BF16)
    for h in range(N_HEADS):
        q_lat = _dot(q_nope[:, h * QK_NOPE:(h + 1) * QK_NOPE], wukt_ref[h]) * SCORE_SCALE
        qlat_ref[h] = q_lat.astype(_BF16)
        qrope_ref[h] = q_rope[:, h * QK_ROPE:(h + 1) * QK_ROPE]


def _inproj(x, w_in, g_q, g_kv, w_q, w_ukt, cs, cc, ss, *, tm):
    t, d = x.shape
    q_rank = g_q.shape[1]
    kv_rank = g_kv.shape[1]
    pool_width = w_in.shape[1] - q_rank - kv_rank - 2 * QK_ROPE
    nper = cs.shape[0] // tm
    const2 = lambda i: (0, 0)
    row = lambda i: (i, 0)
    tab = lambda i: (i % nper, 0)
    head_row = lambda i: (0, i, 0)
    return pl.pallas_call(
        functools.partial(_inproj_kernel, q_rank=q_rank, kv_rank=kv_rank),
        grid=(t // tm,),
        in_specs=[
            pl.BlockSpec((tm, d), row),
            pl.BlockSpec(w_in.shape, const2),
            pl.BlockSpec((1, q_rank), const2),
            pl.BlockSpec((1, kv_rank), const2),
            pl.BlockSpec(w_q.shape, const2),
            pl.BlockSpec(w_ukt.shape, lambda i: (0, 0, 0)),
            pl.BlockSpec((tm, cs.shape[1]), tab),
            pl.BlockSpec((tm, cc.shape[1]), tab),
            pl.BlockSpec((tm, ss.shape[1]), tab),
        ],
        out_specs=[
            pl.BlockSpec((N_HEADS, tm, kv_rank), head_row),
            pl.BlockSpec((N_HEADS, tm, QK_ROPE), head_row),
            pl.BlockSpec((tm, kv_rank), row),
            pl.BlockSpec((tm, QK_ROPE), row),
            pl.BlockSpec((tm, kv_rank), row),
            pl.BlockSpec((tm, QK_ROPE), row),
            pl.BlockSpec((tm, pool_width), row),
        ],
        out_shape=[
            jax.ShapeDtypeStruct((N_HEADS, t, kv_rank), _BF16),
            jax.ShapeDtypeStruct((N_HEADS, t, QK_ROPE), _BF16),
            jax.ShapeDtypeStruct((t, kv_rank), _F32),
            jax.ShapeDtypeStruct((t, QK_ROPE), _F32),
            jax.ShapeDtypeStruct((t, kv_rank), _BF16),
            jax.ShapeDtypeStruct((t, QK_ROPE), _BF16),
            jax.ShapeDtypeStruct((t, pool_width), _F32),
        ],
        compiler_params=_params(("parallel",)),
        name="inproj",
    )(x, w_in, g_q, g_kv, w_q, w_ukt, cs, cc, ss)


def _attn_prompt_kernel(ql_ref, qr_ref, kv_ref, kr_ref, o_ref, s0_ref, s1_ref, mx0_ref, mx1_ref,
                        m_ref, l_ref, acc_ref, *, tq, tk, groups):
    i = pl.program_id(1)
    rows = N_HEADS * tq
    gr = rows // groups
    hg = N_HEADS // groups
    m_ref[...] = jnp.full_like(m_ref, -jnp.inf)
    l_ref[...] = jnp.zeros_like(l_ref)
    acc_ref[...] = jnp.zeros_like(acc_ref)

    def scores(j, s_ref, mx_ref):
        start = pl.multiple_of(j * tk, tk)
        kv = kv_ref[pl.ds(start, tk), :]
        kr = kr_ref[pl.ds(start, tk), :]
        for g in range(groups):
            rs = slice(g * gr, (g + 1) * gr)
            ql = ql_ref[g * hg:(g + 1) * hg].reshape(gr, ql_ref.shape[2])
            qr = qr_ref[g * hg:(g + 1) * hg].reshape(gr, qr_ref.shape[2])
            s = _dot_nt(ql, kv) + _dot_nt(qr, kr)
            s_ref[rs, :] = s
            mx_ref[rs, :] = jnp.max(s, axis=1, keepdims=True)

    def consume(j, s_ref, mx_ref, masked):
        start = pl.multiple_of(j * tk, tk)
        kv = kv_ref[pl.ds(start, tk), :]
        for g in range(groups):
            rs = slice(g * gr, (g + 1) * gr)
            s = s_ref[rs, :]
            if masked:
                q_tok = lax.broadcasted_iota(jnp.int32, (gr, 1), 0) % tq + i * tq
                k_tok = lax.broadcasted_iota(jnp.int32, (1, tk), 1) + start
                s = jnp.where(k_tok // CHUNK <= q_tok // CHUNK, s, -jnp.inf)
                mx = jnp.max(s, axis=1, keepdims=True)
            else:
                mx = mx_ref[rs, :]
            m_old = m_ref[rs, :]
            m_new = jnp.maximum(m_old, mx)
            p = jnp.exp2(s - m_new)
            a = jnp.exp2(m_old - m_new)
            l_ref[rs, :] = a * l_ref[rs, :] + jnp.sum(p, axis=1, keepdims=True)
            acc_ref[rs, :] = a * acc_ref[rs, :] + _dot(p.astype(_BF16), kv)
            m_ref[rs, :] = m_new

    n_full = (i * tq) // tk
    scores(0, s0_ref, mx0_ref)

    def pair(k, carry):
        j = 2 * k
        scores(j + 1, s1_ref, mx1_ref)
        consume(j, s0_ref, mx0_ref, masked=False)
        scores(j + 2, s0_ref, mx0_ref)
        consume(j + 1, s1_ref, mx1_ref, masked=False)
        return carry

    lax.fori_loop(0, n_full // 2, pair, 0)

    @pl.when(n_full % 2 == 0)
    def _():
        consume(n_full, s0_ref, mx0_ref, masked=True)

    @pl.when(n_full % 2 == 1)
    def _():
        scores(n_full, s1_ref, mx1_ref)
        consume(n_full - 1, s0_ref, mx0_ref, masked=False)
        consume(n_full, s1_ref, mx1_ref, masked=True)

    o = acc_ref[...] / l_ref[...]
    o_ref[...] = o.reshape(o_ref.shape).astype(o_ref.dtype)


def _attn_prompt(qlat, qrope, ckvb, krb, *, batch, seq, tq, tk, groups):
    assert tk % tq == 0 and tq % CHUNK == 0 and seq % tk == 0 and N_HEADS % groups == 0
    h, t, c = qlat.shape
    r = qrope.shape[2]
    nq = seq // tq
    rows = h * tq
    head_row = lambda b, i: (0, b * nq + i, 0)
    return pl.pallas_call(
        functools.partial(_attn_prompt_kernel, tq=tq, tk=tk, groups=groups),
        grid=(batch, nq),
        in_specs=[
            pl.BlockSpec((h, tq, c), head_row),
            pl.BlockSpec((h, tq, r), head_row),
            pl.BlockSpec((seq, c), lambda b, i: (b, 0)),
            pl.BlockSpec((seq, r), lambda b, i: (b, 0)),
        ],
        out_specs=pl.BlockSpec((h, tq, c), head_row),
        out_shape=jax.ShapeDtypeStruct((h, t, c), _BF16),
        scratch_shapes=[
            pltpu.VMEM((rows, tk), _F32),
            pltpu.VMEM((rows, tk), _F32),
            pltpu.VMEM((rows, 1), _F32),
            pltpu.VMEM((rows, 1), _F32),
            pltpu.VMEM((rows, 1), _F32),
            pltpu.VMEM((rows, 1), _F32),
            pltpu.VMEM((rows, c), _F32),
        ],
        compiler_params=_params(("parallel", "arbitrary")),
        name="attn_prompt",
    )(qlat, qrope, ckvb, krb)


def _attn_sample_kernel(ql_ref, qr_ref, pkv_ref, pkr_ref, kv_ref, kr_ref, o_ref):
    rows = ql_ref.shape[0] * ql_ref.shape[1]
    ql = ql_ref[...].reshape(rows, ql_ref.shape[2])
    qr = qr_ref[...].reshape(rows, qr_ref.shape[2])
    pkv = pkv_ref[0].astype(_BF16)
    pkr = pkr_ref[0].astype(_BF16)
    kv = kv_ref[...]
    kr = kr_ref[...]
    s_past = _dot_nt(ql, pkv) + _dot_nt(qr, pkr)
    s_new = _dot_nt(ql, kv) + _dot_nt(qr, kr)
    m = jnp.maximum(jnp.max(s_past, axis=1, keepdims=True), jnp.max(s_new, axis=1, keepdims=True))
    p_past = jnp.exp2(s_past - m)
    p_new = jnp.exp2(s_new - m)
    l = jnp.sum(p_past, axis=1, keepdims=True) + jnp.sum(p_new, axis=1, keepdims=True)
    o = (_dot(p_past.astype(_BF16), pkv) + _dot(p_new.astype(_BF16), kv)) / l
    o_ref[...] = o.reshape(o_ref.shape).astype(o_ref.dtype)


def _attn_sample(qlat, qrope, past_kv, past_kr, ckvb, krb, *, seq):
    h, t, c = qlat.shape
    r = qrope.shape[2]
    batch, past_len, _ = past_kv.shape
    head_row = lambda b: (0, b, 0)
    return pl.pallas_call(
        _attn_sample_kernel,
        grid=(batch,),
        in_specs=[
            pl.BlockSpec((h, seq, c), head_row),
            pl.BlockSpec((h, seq, r), head_row),
            pl.BlockSpec((1, past_len, c), lambda b: (b, 0, 0)),
            pl.BlockSpec((1, past_len, r), lambda b: (b, 0, 0)),
            pl.BlockSpec((seq, c), lambda b: (b, 0)),
            pl.BlockSpec((seq, r), lambda b: (b, 0)),
        ],
        out_specs=pl.BlockSpec((h, seq, c), head_row),
        out_shape=jax.ShapeDtypeStruct((h, t, c), _BF16),
        compiler_params=_params(("parallel",)),
        name="attn_sample",
    )(qlat, qrope, past_kv, past_kr, ckvb, krb)


def _pool_delta(ext, pos0, group):
    n = ext.shape[0] - HALO
    pos = pos0 + lax.broadcasted_iota(jnp.int32, (n, 1), 0)
    out = []
    for gi, w in enumerate(POOL_WINDOWS):
        cols = ext[:, gi * group:(gi + 1) * group]
        s = cols
        shift = 1
        while shift < w:
            s = s + pltpu.roll(s, shift, axis=0)
            shift *= 2
        cnt = jnp.minimum(w, pos + 1).astype(_F32)
        out.append(s[HALO:] / cnt - cols[HALO:])
    return out


def _outproj_kernel(olat_ref, u_ref, halo_ref, x_ref, wuv_ref, wpool_ref, pscale_ref, wo_ref,
                    g_ref, b_ref, o_ref, *, alpha, seq, offset, blocks_per_seq):
    tm = u_ref.shape[0]
    group = u_ref.shape[1] // len(POOL_WINDOWS)
    u = u_ref[...]
    halo = halo_ref[...]
    if blocks_per_seq is not None:
        bi = pl.program_id(0) % blocks_per_seq
        halo = jnp.where(bi == 0, 0.0, halo)
        deltas = [_pool_delta(jnp.concatenate([halo, u], axis=0), offset + bi * tm, group)]
    else:
        deltas = []
        for s in range(tm // seq):
            ext = jnp.concatenate(
                [halo[s * HALO:(s + 1) * HALO], u[s * seq:(s + 1) * seq]], axis=0)
            deltas.append(_pool_delta(ext, offset, group))
    pooled = []
    for gi in range(len(POOL_WINDOWS)):
        d = jnp.concatenate([ds[gi] for ds in deltas], axis=0) if len(deltas) > 1 else deltas[0][gi]
        pooled.append(_dot(d.astype(_BF16), wpool_ref[gi]))
    o_pool = jnp.concatenate(pooled, axis=1) * pscale_ref[...]
    o_attn = jnp.concatenate(
        [_dot(olat_ref[h], wuv_ref[h]) for h in range(N_HEADS)], axis=1)
    mixed = jnp.concatenate([o_attn, o_pool], axis=1).astype(_BF16)
    y = alpha * x_ref[...] + _dot(mixed, wo_ref[...])
    o_ref[...] = _layer_norm(y, g_ref[...], b_ref[...])


def _outproj(olat, u, halo_src, x, w_uv, w_pool, pool_scale, w_o, g, b,
             *, alpha, seq, offset, tm, prompt):
    t, d = x.shape
    h, _, c = olat.shape
    pw = u.shape[1]
    row = lambda i: (i, 0)
    const2 = lambda i: (0, 0)
    const3 = lambda i: (0, 0, 0)
    if prompt:
        blocks_per_seq = seq // tm
        step = tm // HALO
        halo_spec = pl.BlockSpec((HALO, pw), lambda i: (jnp.maximum(i * step - 1, 0), 0))
    else:
        blocks_per_seq = None
        halo_spec = pl.BlockSpec(((tm // seq) * HALO, pw), row)
    return pl.pallas_call(
        functools.partial(_outproj_kernel, alpha=alpha, seq=seq, offset=offset,
                          blocks_per_seq=blocks_per_seq),
        grid=(t // tm,),
        in_specs=[
            pl.BlockSpec((h, tm, c), lambda i: (0, i, 0)),
            pl.BlockSpec((tm, pw), row),
            halo_spec,
            pl.BlockSpec((tm, d), row),
            pl.BlockSpec(w_uv.shape, const3),
            pl.BlockSpec(w_pool.shape, const3),
            pl.BlockSpec((1, pw), const2),
            pl.BlockSpec(w_o.shape, const2),
            pl.BlockSpec((1, d), const2),
            pl.BlockSpec((1, d), const2),
        ],
        out_specs=pl.BlockSpec((tm, d), row),
        out_shape=jax.ShapeDtypeStruct((t, d), _F32),
        compiler_params=_params(("parallel",)),
        name="outproj_prompt" if prompt else "outproj_sample",
    )(olat, u, halo_src, x, w_uv, w_pool, pool_scale, w_o, g, b)


def _ple_kernel(x_ref, p_ref, wg_ref, wp_ref, g_ref, b_ref, o_ref, *, alpha):
    x = x_ref[...]
    gate = jax.nn.sigmoid(_dot(x.astype(_BF16), wg_ref[...]))
    proj = _dot(p_ref[...].astype(_BF16), wp_ref[...])
    o_ref[...] = _layer_norm(alpha * x + gate * proj, g_ref[...], b_ref[...])


def _ple(x, p, w_gate, w_proj, g, b, *, alpha, tm):
    t, d = x.shape
    pd = p.shape[1]
    row = lambda i: (i, 0)
    const2 = lambda i: (0, 0)
    return pl.pallas_call(
        functools.partial(_ple_kernel, alpha=alpha),
        grid=(t // tm,),
        in_specs=[
            pl.BlockSpec((tm, d), row),
            pl.BlockSpec((tm, pd), row),
            pl.BlockSpec(w_gate.shape, const2),
            pl.BlockSpec(w_proj.shape, const2),
            pl.BlockSpec((1, d), const2),
            pl.BlockSpec((1, d), const2),
        ],
        out_specs=pl.BlockSpec((tm, d), row),
        out_shape=jax.ShapeDtypeStruct((t, d), _F32),
        compiler_params=_params(("parallel",)),
        name="ple",
    )(x, p, w_gate, w_proj, g, b)


def _rotate_half_cols(w):
    half = w.shape[-1] // 2
    return jnp.concatenate([-w[..., half:], w[..., :half]], axis=-1)


def _rope_tables(pos, reps):
    inv = 1.0 / (ROPE_THETA ** (jnp.arange(0, QK_ROPE, 2, dtype=_F32) / QK_ROPE))
    ang = pos.astype(_F32)[:, None] * inv[None, :]
    ang = jnp.concatenate([ang, ang], -1)
    cos, sin = jnp.cos(ang), jnp.sin(ang)
    tile = lambda a: jnp.tile(a, (reps, 1))
    return (tile(jnp.concatenate([cos, sin], -1)), tile(jnp.concatenate([cos, cos], -1)),
            tile(jnp.concatenate([sin, sin], -1)))


def _tile_rows(total, preferred):
    tile = min(total, preferred)
    assert total % tile == 0
    return tile


def _prep_layer_weights(w):
    q_rank = w['g_q'].shape[0]
    kv_rank = w['g_kv'].shape[0]
    w_in = w['w_in']
    o_kr = q_rank + kv_rank
    w_kr = w_in[:, o_kr:o_kr + QK_ROPE]
    w_in_all = jnp.concatenate(
        [w_in[:, :o_kr], w_kr, _rotate_half_cols(w_kr), w_in[:, o_kr + QK_ROPE:]], axis=1)
    w_q = jnp.concatenate(
        [w['w_uq_nope'].reshape(q_rank, -1), w['w_uq_rope'].reshape(q_rank, -1),
         _rotate_half_cols(w['w_uq_rope']).reshape(q_rank, -1)], axis=1)
    row = lambda v: v.reshape(1, -1)
    return dict(
        ffn1_gu=w['ffn1_gu'].astype(_BF16), ffn1_down=w['ffn1_down'].astype(_BF16),
        ffn2_gu=w['ffn2_gu'].astype(_BF16), ffn2_down=w['ffn2_down'].astype(_BF16),
        w_in=w_in_all.astype(_BF16), w_q=w_q.astype(_BF16),
        w_ukt=jnp.transpose(w['w_uk'], (1, 2, 0)).astype(_BF16),
        w_uv=jnp.transpose(w['w_uv'], (1, 0, 2)).astype(_BF16),
        w_pool=w['w_pool'].astype(_BF16), pool_scale=row(w['pool_scale']),
        w_o=w['w_o'].astype(_BF16),
        w_ple_gate=w['w_ple_gate'].astype(_BF16), w_ple_proj=w['w_ple_proj'].astype(_BF16),
        g_q=row(w['g_q']), g_kv=row(w['g_kv']),
        ln1=(row(w['ln1_g']), row(w['ln1_b'])), ln2=(row(w['ln2_g']), row(w['ln2_b'])),
        ln3=(row(w['ln3_g']), row(w['ln3_b'])), ln4=(row(w['ln4_g']), row(w['ln4_b'])),
    )


def _encoder_layer(x, p, past, pool_hist, offset, pw, *, alpha):
    bsz, seq, d = x.shape
    t = bsz * seq
    prompt = past is None
    xf = x.reshape(t, d)
    tm = _tile_rows(t, 512)
    d_ff = pw['ffn1_down'].shape[0]
    tf = 512 if d_ff % 512 == 0 else d_ff

    x1 = _ffn_ln(xf, pw['ffn1_gu'], pw['ffn1_down'], *pw['ln1'], alpha=alpha, tm=tm, tf=tf)

    tmi = _tile_rows(t, 256)
    if prompt:
        assert seq % tmi == 0
        tabs = _rope_tables(offset + jnp.arange(seq), 1)
    else:
        assert tmi % seq == 0
        tabs = _rope_tables(offset + jnp.arange(seq), tmi // seq)
    qlat, qrope, ckv, kr, ckvb, krb, u = _inproj(
        x1, pw['w_in'], pw['g_q'], pw['g_kv'], pw['w_q'], pw['w_ukt'], *tabs, tm=tmi)

    if prompt:
        tq = min(seq, 128)
        tk = min(seq, 512)
        olat = _attn_prompt(qlat, qrope, ckvb, krb, batch=bsz, seq=seq, tq=tq, tk=tk,
                            groups=2)
        halo_src = u
    else:
        olat = _attn_sample(qlat, qrope, past[0], past[1], ckvb, krb, seq=seq)
        halo_src = jnp.pad(pool_hist, ((0, 0), (HALO - POOL_HIST, 0), (0, 0))).reshape(bsz * HALO, -1)

    tmo = _tile_rows(t, 512 if prompt else 256)
    if prompt:
        assert seq % tmo == 0 and tmo % HALO == 0
    else:
        assert tmo % seq == 0
    x2 = _outproj(olat, u, halo_src, x1, pw['w_uv'], pw['w_pool'], pw['pool_scale'], pw['w_o'],
                  *pw['ln2'], alpha=alpha, seq=seq, offset=offset, tm=tmo, prompt=prompt)
    x3 = _ffn_ln(x2, pw['ffn2_gu'], pw['ffn2_down'], *pw['ln3'], alpha=alpha, tm=tm, tf=tf)
    y = _ple(x3, p.reshape(t, -1), pw['w_ple_gate'], pw['w_ple_proj'], *pw['ln4'],
             alpha=alpha, tm=tm)

    u3 = u.reshape(bsz, seq, -1)
    if seq >= POOL_HIST:
        new_hist = u3[:, seq - POOL_HIST:]
    else:
        new_hist = jnp.concatenate([pool_hist, u3], axis=1)[:, -POOL_HIST:]
    return (y.reshape(bsz, seq, d), ckv.reshape(bsz, seq, -1), kr.reshape(bsz, seq, -1), new_hist)


def kernel(x_prompt, x_sample, cache_kv_latent, cache_k_rope, state_pool, p_prompt, p_sample, ffn1_gu, ffn1_down, ln1_g, ln1_b, w_in, g_q, g_kv, w_uq_nope, w_uq_rope, w_uk, w_uv, w_pool, pool_scale, w_o, ln2_g, ln2_b, ffn2_gu, ffn2_down, ln3_g, ln3_b, w_ple_gate, w_ple_proj, ln4_g, ln4_b):
    depth = ffn1_gu.shape[0]
    past_len = cache_kv_latent.shape[2]
    alpha = (2.0 * depth) ** 0.25
    stacked = dict(
        ffn1_gu=ffn1_gu, ffn1_down=ffn1_down, ln1_g=ln1_g, ln1_b=ln1_b, w_in=w_in, g_q=g_q,
        g_kv=g_kv, w_uq_nope=w_uq_nope, w_uq_rope=w_uq_rope, w_uk=w_uk, w_uv=w_uv,
        w_pool=w_pool, pool_scale=pool_scale, w_o=w_o, ln2_g=ln2_g, ln2_b=ln2_b,
        ffn2_gu=ffn2_gu, ffn2_down=ffn2_down, ln3_g=ln3_g, ln3_b=ln3_b,
        w_ple_gate=w_ple_gate, w_ple_proj=w_ple_proj, ln4_g=ln4_g, ln4_b=ln4_b)
    hp, hs = x_prompt, x_sample
    outs = [[] for _ in range(6)]
    for i in range(depth):
        pw = _prep_layer_weights({k: v[i] for k, v in stacked.items()})
        hp, c1, k1, s1 = _encoder_layer(hp, p_prompt[i], None, None, 0, pw, alpha=alpha)
        hs, c2, k2, s2 = _encoder_layer(
            hs, p_sample[i], (cache_kv_latent[i], cache_k_rope[i]), state_pool[i], past_len, pw,
            alpha=alpha)
        for acc, v in zip(outs, (c1, k1, s1, c2, k2, s2)):
            acc.append(v)
    return (hp, hs) + tuple(jnp.stack(v) for v in outs)
```

```python
import functools
import math

import jax
import jax.numpy as jnp
from jax import lax
from jax.experimental import pallas as pl
from jax.experimental.pallas import tpu as pltpu

CHUNK = 64
N_HEADS = 8
QK_NOPE = 128
QK_ROPE = 64
V_HEAD = 128
POOL_WINDOWS = (2, 4, 8, 16)
POOL_HIST = max(POOL_WINDOWS) - 1
HALO = POOL_HIST + 1
ROPE_THETA = 10000.0
LN_EPS = 1e-5
RMS_EPS = 1e-6
ATTN_SCALE = (QK_NOPE + QK_ROPE) ** -0.5
SCORE_SCALE = ATTN_SCALE * math.log2(math.e)

V7X_VMEM_LIMIT_BYTES = 56 * 1024 * 1024
STAT_LANES = 128

_F32 = jnp.float32
_BF16 = jnp.bfloat16
_NT_DIMS = (((1,), (1,)), ((), ()))


def _dot(a, b):
    return jnp.dot(a, b, preferred_element_type=_F32)


def _dot_nt(a, b):
    return lax.dot_general(a, b, _NT_DIMS, preferred_element_type=_F32)


def _layer_norm(y, g, b):
    mu = jnp.mean(y, axis=-1, keepdims=True)
    d = y - mu
    var = jnp.mean(d * d, axis=-1, keepdims=True)
    return d * lax.rsqrt(var + LN_EPS) * g + b


def _rms_norm(z, g):
    return z * lax.rsqrt(jnp.mean(z * z, axis=-1, keepdims=True) + RMS_EPS) * g


def _row_groups(rows, group_rows=256):
    group_rows = min(rows, group_rows)
    return [slice(r, r + group_rows) for r in range(0, rows, group_rows)]


def _params(semantics):
    return pltpu.CompilerParams(
        dimension_semantics=semantics, vmem_limit_bytes=V7X_VMEM_LIMIT_BYTES)


def _ffn_ln_kernel(x_ref, wg_ref, wu_ref, wd_ref, g_ref, b_ref, o_ref, xb_ref, *, alpha):
    j = pl.program_id(1)

    @pl.when(j == 0)
    def _():
        xb_ref[...] = x_ref[...].astype(_BF16)
        o_ref[...] = jnp.zeros_like(o_ref)

    xb = xb_ref[...]
    gate = _dot(xb, wg_ref[...])
    up = _dot(xb, wu_ref[...])
    act = (gate * jax.nn.sigmoid(gate) * up).astype(_BF16)
    o_ref[...] += _dot(act, wd_ref[...])

    @pl.when(j == pl.num_programs(1) - 1)
    def _():
        y = alpha * x_ref[...] + 0.5 * o_ref[...]
        o_ref[...] = _layer_norm(y, g_ref[...], b_ref[...])


def _ffn_ln(x, w_gu, w_down, g, b, *, alpha, tm, tf):
    t, d = x.shape
    f = w_down.shape[0]
    nj = f // tf
    return pl.pallas_call(
        functools.partial(_ffn_ln_kernel, alpha=alpha),
        grid=(t // tm, nj),
        in_specs=[
            pl.BlockSpec((tm, d), lambda i, j: (i, 0)),
            pl.BlockSpec((d, tf), lambda i, j: (0, j)),
            pl.BlockSpec((d, tf), lambda i, j: (0, j + nj)),
            pl.BlockSpec((tf, d), lambda i, j: (j, 0)),
            pl.BlockSpec((1, d), lambda i, j: (0, 0)),
            pl.BlockSpec((1, d), lambda i, j: (0, 0)),
        ],
        out_specs=pl.BlockSpec((tm, d), lambda i, j: (i, 0)),
        out_shape=jax.ShapeDtypeStruct((t, d), _F32),
        scratch_shapes=[pltpu.VMEM((tm, d), _BF16)],
        compiler_params=_params(("parallel", "arbitrary")),
        name="ffn_ln",
    )(x, w_gu, w_gu, w_down, g, b)


def _inproj_kernel(x_ref, win_ref, gq_ref, gkv_ref, wq_ref, wuk_ref, cs_ref, cc_ref, ss_ref,
                   *out_refs, q_rank, kv_rank, absorbed):
    if absorbed:
        qlat_ref, qrope_ref, ckv_ref, kr_ref, ckvb_ref, krb_ref, u_ref = out_refs
    else:
        qcat_ref, kcat_ref, ckv_ref, kr_ref, ckvb_ref, u_ref = out_refs
    xb = x_ref[...].astype(_BF16)
    z = _dot(xb, win_ref[...])
    o_kv = q_rank
    o_kr = o_kv + kv_rank
    o_u = o_kr + 2 * QK_ROPE
    u_ref[...] = z[:, o_u:]

    c_kv = _rms_norm(z[:, o_kv:o_kr], gkv_ref[...])
    c_kv_b = c_kv.astype(_BF16)
    ckv_ref[...] = c_kv
    ckvb_ref[...] = c_kv_b

    kt = z[:, o_kr:o_u] * cs_ref[...]
    k2 = kt + pltpu.roll(kt, QK_ROPE, axis=1)
    kr_ref[...] = k2[:, :QK_ROPE]

    c_q = _rms_norm(z[:, :q_rank], gq_ref[...]).astype(_BF16)
    qz = _dot(c_q, wq_ref[...])
    n_nope = N_HEADS * QK_NOPE
    n_rope = N_HEADS * QK_ROPE
    lane_reps = n_rope // cc_ref.shape[1]
    cos = jnp.concatenate([cc_ref[...]] * lane_reps, axis=1)
    sin = jnp.concatenate([ss_ref[...]] * lane_reps, axis=1)
    q_rope = (qz[:, n_nope:n_nope + n_rope] * cos + qz[:, n_nope + n_rope:] * sin) * SCORE_SCALE
    if absorbed:
        krb_ref[...] = k2[:, :QK_ROPE].astype(_BF16)
        q_rope = q_rope.astype(_BF16)
        q_nope = qz[:, :n_nope].astype(_BF16)
        for h in range(N_HEADS):
            q_lat = _dot(q_nope[:, h * QK_NOPE:(h + 1) * QK_NOPE], wuk_ref[h]) * SCORE_SCALE
            qlat_ref[h] = q_lat.astype(_BF16)
            qrope_ref[h] = q_rope[:, h * QK_ROPE:(h + 1) * QK_ROPE]
    else:
        pair = 2 * QK_ROPE
        low = lax.broadcasted_iota(jnp.int32, (x_ref.shape[0], pair), 1) < QK_ROPE
        k_tail = jnp.where(low, k2, 0.0).astype(_BF16)
        q_nope = (qz[:, :n_nope] * SCORE_SCALE).astype(_BF16)
        k_nope = _dot(c_kv_b, wuk_ref[...]).astype(_BF16)
        for h in range(N_HEADS):
            v = q_rope[:, (h // 2) * pair:(h // 2 + 1) * pair]
            if h % 2:
                v = pltpu.roll(v, QK_ROPE, axis=1)
            q_tail = jnp.where(low, v, 0.0).astype(_BF16)
            hs = slice(h * QK_NOPE, (h + 1) * QK_NOPE)
            qcat_ref[h] = jnp.concatenate([q_nope[:, hs], q_tail], axis=1)
            kcat_ref[h] = jnp.concatenate([k_nope[:, hs], k_tail], axis=1)


def _inproj(x, w_in, g_q, g_kv, w_q, w_uk, cs, cc, ss, *, tm, absorbed):
    t, d = x.shape
    q_rank = g_q.shape[1]
    kv_rank = g_kv.shape[1]
    pool_width = w_in.shape[1] - q_rank - kv_rank - 2 * QK_ROPE
    nper = cs.shape[0] // tm
    const2 = lambda i: (0, 0)
    row = lambda i: (i, 0)
    tab = lambda i: (i % nper, 0)
    head_row = lambda i: (0, i, 0)

    def rows_out(width, dtype):
        return pl.BlockSpec((tm, width), row), jax.ShapeDtypeStruct((t, width), dtype)

    def heads_out(width):
        return (pl.BlockSpec((N_HEADS, tm, width), head_row),
                jax.ShapeDtypeStruct((N_HEADS, t, width), _BF16))

    cat = QK_NOPE + 2 * QK_ROPE
    if absorbed:
        outs = [heads_out(kv_rank), heads_out(QK_ROPE)]
    else:
        outs = [heads_out(cat), heads_out(cat)]
    outs += [rows_out(kv_rank, _F32), rows_out(QK_ROPE, _F32), rows_out(kv_rank, _BF16)]
    if absorbed:
        outs.append(rows_out(QK_ROPE, _BF16))
    outs.append(rows_out(pool_width, _F32))
    return pl.pallas_call(
        functools.partial(_inproj_kernel, q_rank=q_rank, kv_rank=kv_rank, absorbed=absorbed),
        grid=(t // tm,),
        in_specs=[
            pl.BlockSpec((tm, d), row),
            pl.BlockSpec(w_in.shape, const2),
            pl.BlockSpec((1, q_rank), const2),
            pl.BlockSpec((1, kv_rank), const2),
            pl.BlockSpec(w_q.shape, const2),
            pl.BlockSpec(w_uk.shape, lambda i: (0,) * w_uk.ndim),
            pl.BlockSpec((tm, cs.shape[1]), tab),
            pl.BlockSpec((tm, cc.shape[1]), tab),
            pl.BlockSpec((tm, ss.shape[1]), tab),
        ],
        out_specs=[o[0] for o in outs],
        out_shape=[o[1] for o in outs],
        compiler_params=_params(("parallel",)),
        name="inproj",
    )(x, w_in, g_q, g_kv, w_q, w_uk, cs, cc, ss)


def _attn_prompt_kernel(q_ref, k_ref, v_ref, o_ref, s0_ref, s1_ref, mx0_ref, mx1_ref,
                        p0_ref, p1_ref, a0_ref, a1_ref, m_ref, l_ref, acc_ref, *, tq, tk):
    i = pl.program_id(2)
    m_ref[...] = jnp.full_like(m_ref, -jnp.inf)
    l_ref[...] = jnp.zeros_like(l_ref)
    acc_ref[...] = jnp.zeros_like(acc_ref)
    s_bufs = ((s0_ref, mx0_ref), (s1_ref, mx1_ref))
    p_bufs = ((p0_ref, a0_ref), (p1_ref, a1_ref))

    def keys(j):
        return pl.ds(pl.multiple_of(j * tk, tk), tk)

    def lanes(v, width):
        return jnp.concatenate([v] * (width // v.shape[1]), axis=1)

    def row_stat(v):
        return jnp.broadcast_to(v, (v.shape[0], STAT_LANES))

    def scores(j, b):
        s_ref, mx_ref = s_bufs[b]
        s = _dot_nt(q_ref[0], k_ref[0, keys(j), :])
        s_ref[...] = s
        mx_ref[...] = row_stat(jnp.max(s, axis=1, keepdims=True))

    def softmax(j, b, masked=False):
        s_ref, mx_ref = s_bufs[b]
        p_ref, a_ref = p_bufs[b]
        s = s_ref[...]
        if masked:
            q_tok = lax.broadcasted_iota(jnp.int32, (tq, 1), 0) + i * tq
            k_tok = lax.broadcasted_iota(jnp.int32, (1, tk), 1) + j * tk
            s = jnp.where(k_tok // CHUNK <= q_tok // CHUNK, s, -jnp.inf)
            mx = row_stat(jnp.max(s, axis=1, keepdims=True))
        else:
            mx = mx_ref[...]
        m_old = m_ref[...]
        m_new = jnp.maximum(m_old, mx)
        p = jnp.exp2(s - lanes(m_new, tk))
        a = jnp.exp2(m_old - m_new)
        l_ref[...] = a * l_ref[...] + row_stat(jnp.sum(p, axis=1, keepdims=True))
        m_ref[...] = m_new
        a_ref[...] = a
        p_ref[...] = p

    def values(j, b):
        p_ref, a_ref = p_bufs[b]
        acc_ref[...] = (lanes(a_ref[...], acc_ref.shape[1]) * acc_ref[...]
                        + _dot(p_ref[...].astype(_BF16), v_ref[keys(j), :]))

    n_full = (i * tq) // tk
    scores(0, 0)

    @pl.when(n_full == 0)
    def _():
        softmax(0, 0, masked=True)
        values(0, 0)

    @pl.when(n_full > 0)
    def _():
        scores(1, 1)
        softmax(0, 0)

        def pair(k, carry):
            t = 2 * k + 2
            scores(t, 0)
            softmax(t - 1, 1)
            values(t - 2, 0)
            scores(t + 1, 1)
            softmax(t, 0)
            values(t - 1, 1)
            return carry

        lax.fori_loop(0, (n_full - 1) // 2, pair, 0)

        @pl.when(n_full % 2 == 1)
        def _():
            softmax(n_full, 1, masked=True)
            values(n_full - 1, 0)
            values(n_full, 1)

        @pl.when(n_full % 2 == 0)
        def _():
            scores(n_full, 0)
            softmax(n_full - 1, 1)
            values(n_full - 2, 0)
            softmax(n_full, 0, masked=True)
            values(n_full - 1, 1)
            values(n_full, 0)

    o = acc_ref[...] / lanes(l_ref[...], acc_ref.shape[1])
    o_ref[0] = o.astype(o_ref.dtype)


def _attn_prompt(qcat, kcat, ckvb, *, batch, seq, tq, tk):
    assert tk % tq == 0 and tq % CHUNK == 0 and seq % tk == 0
    h, t, e = qcat.shape
    c = ckvb.shape[1]
    nq = seq // tq
    q_block = lambda b, hd, i: (hd, b * nq + i, 0)
    stat = pltpu.VMEM((tq, STAT_LANES), _F32)
    tile = pltpu.VMEM((tq, tk), _F32)
    return pl.pallas_call(
        functools.partial(_attn_prompt_kernel, tq=tq, tk=tk),
        grid=(batch, h, nq),
        in_specs=[
            pl.BlockSpec((1, tq, e), q_block),
            pl.BlockSpec((1, seq, e), lambda b, hd, i: (hd, b, 0)),
            pl.BlockSpec((seq, c), lambda b, hd, i: (b, 0)),
        ],
        out_specs=pl.BlockSpec((1, tq, c), q_block),
        out_shape=jax.ShapeDtypeStruct((h, t, c), _BF16),
        scratch_shapes=[
            tile, tile,
            stat, stat,
            tile, tile,
            stat, stat,
            stat,
            stat,
            pltpu.VMEM((tq, c), _F32),
        ],
        compiler_params=_params(("parallel", "parallel", "arbitrary")),
        name="attn_prompt",
    )(qcat, kcat, ckvb)


def _attn_sample_kernel(ql_ref, qr_ref, pkv_ref, pkr_ref, kv_ref, kr_ref, o_ref):
    rows = ql_ref.shape[0] * ql_ref.shape[1]
    ql = ql_ref[...].reshape(rows, ql_ref.shape[2])
    qr = qr_ref[...].reshape(rows, qr_ref.shape[2])
    pkv = pkv_ref[0].astype(_BF16)
    pkr = pkr_ref[0].astype(_BF16)
    kv = kv_ref[...]
    kr = kr_ref[...]
    s_past = _dot_nt(ql, pkv) + _dot_nt(qr, pkr)
    s_new = _dot_nt(ql, kv) + _dot_nt(qr, kr)
    m = jnp.maximum(jnp.max(s_past, axis=1, keepdims=True), jnp.max(s_new, axis=1, keepdims=True))
    p_past = jnp.exp2(s_past - m)
    p_new = jnp.exp2(s_new - m)
    l = jnp.sum(p_past, axis=1, keepdims=True) + jnp.sum(p_new, axis=1, keepdims=True)
    o = (_dot(p_past.astype(_BF16), pkv) + _dot(p_new.astype(_BF16), kv)) / l
    o_ref[...] = o.reshape(o_ref.shape).astype(o_ref.dtype)


def _attn_sample(qlat, qrope, past_kv, past_kr, ckvb, krb, *, seq):
    h, t, c = qlat.shape
    r = qrope.shape[2]
    batch, past_len, _ = past_kv.shape
    head_row = lambda b: (0, b, 0)
    return pl.pallas_call(
        _attn_sample_kernel,
        grid=(batch,),
        in_specs=[
            pl.BlockSpec((h, seq, c), head_row),
            pl.BlockSpec((h, seq, r), head_row),
            pl.BlockSpec((1, past_len, c), lambda b: (b, 0, 0)),
            pl.BlockSpec((1, past_len, r), lambda b: (b, 0, 0)),
            pl.BlockSpec((seq, c), lambda b: (b, 0)),
            pl.BlockSpec((seq, r), lambda b: (b, 0)),
        ],
        out_specs=pl.BlockSpec((h, seq, c), head_row),
        out_shape=jax.ShapeDtypeStruct((h, t, c), _BF16),
        compiler_params=_params(("parallel",)),
        name="attn_sample",
    )(qlat, qrope, past_kv, past_kr, ckvb, krb)


def _pool_delta(ext, pos0, group):
    n = ext.shape[0] - HALO
    pos = pos0 + lax.broadcasted_iota(jnp.int32, (n, 1), 0)
    out = []
    for gi, w in enumerate(POOL_WINDOWS):
        cols = ext[:, gi * group:(gi + 1) * group]
        s = cols
        shift = 1
        while shift < w:
            s = s + pltpu.roll(s, shift, axis=0)
            shift *= 2
        cnt = jnp.minimum(w, pos + 1).astype(_F32)
        out.append(s[HALO:] / cnt - cols[HALO:])
    return out


def _outproj_kernel(olat_ref, u_ref, halo_ref, x_ref, wuv_ref, wpool_ref, pscale_ref, wo_ref,
                    g_ref, b_ref, o_ref, *, alpha, seq, offset, blocks_per_seq):
    tm = u_ref.shape[0]
    group = u_ref.shape[1] // len(POOL_WINDOWS)
    u = u_ref[...]
    halo = halo_ref[...]
    if blocks_per_seq is not None:
        bi = pl.program_id(0) % blocks_per_seq
        halo = jnp.where(bi == 0, 0.0, halo)
        deltas = [_pool_delta(jnp.concatenate([halo, u], axis=0), offset + bi * tm, group)]
    else:
        deltas = []
        for s in range(tm // seq):
            ext = jnp.concatenate(
                [halo[s * HALO:(s + 1) * HALO], u[s * seq:(s + 1) * seq]], axis=0)
            deltas.append(_pool_delta(ext, offset, group))
    pooled = []
    for gi in range(len(POOL_WINDOWS)):
        d = jnp.concatenate([ds[gi] for ds in deltas], axis=0) if len(deltas) > 1 else deltas[0][gi]
        pooled.append(_dot(d.astype(_BF16), wpool_ref[gi]))
    o_pool = jnp.concatenate(pooled, axis=1) * pscale_ref[...]
    o_attn = jnp.concatenate(
        [_dot(olat_ref[h], wuv_ref[h]) for h in range(N_HEADS)], axis=1)
    mixed = jnp.concatenate([o_attn, o_pool], axis=1).astype(_BF16)
    for rs in _row_groups(tm):
        y = alpha * x_ref[rs, :] + _dot(mixed[rs], wo_ref[...])
        o_ref[rs, :] = _layer_norm(y, g_ref[...], b_ref[...])


def _outproj(olat, u, halo_src, x, w_uv, w_pool, pool_scale, w_o, g, b,
             *, alpha, seq, offset, tm, prompt):
    t, d = x.shape
    h, _, c = olat.shape
    pw = u.shape[1]
    row = lambda i: (i, 0)
    const2 = lambda i: (0, 0)
    const3 = lambda i: (0, 0, 0)
    if prompt:
        blocks_per_seq = seq // tm
        step = tm // HALO
        halo_spec = pl.BlockSpec((HALO, pw), lambda i: (jnp.maximum(i * step - 1, 0), 0))
    else:
        blocks_per_seq = None
        halo_spec = pl.BlockSpec(((tm // seq) * HALO, pw), row)
    return pl.pallas_call(
        functools.partial(_outproj_kernel, alpha=alpha, seq=seq, offset=offset,
                          blocks_per_seq=blocks_per_seq),
        grid=(t // tm,),
        in_specs=[
            pl.BlockSpec((h, tm, c), lambda i: (0, i, 0)),
            pl.BlockSpec((tm, pw), row),
            halo_spec,
            pl.BlockSpec((tm, d), row),
            pl.BlockSpec(w_uv.shape, const3),
            pl.BlockSpec(w_pool.shape, const3),
            pl.BlockSpec((1, pw), const2),
            pl.BlockSpec(w_o.shape, const2),
            pl.BlockSpec((1, d), const2),
            pl.BlockSpec((1, d), const2),
        ],
        out_specs=pl.BlockSpec((tm, d), row),
        out_shape=jax.ShapeDtypeStruct((t, d), _F32),
        compiler_params=_params(("parallel",)),
        name="outproj_prompt" if prompt else "outproj_sample",
    )(olat, u, halo_src, x, w_uv, w_pool, pool_scale, w_o, g, b)


def _ple_kernel(x_ref, p_ref, wg_ref, wp_ref, g_ref, b_ref, o_ref, *, alpha):
    for rs in _row_groups(x_ref.shape[0]):
        x = x_ref[rs, :]
        gate = jax.nn.sigmoid(_dot(x.astype(_BF16), wg_ref[...]))
        proj = _dot(p_ref[rs, :].astype(_BF16), wp_ref[...])
        o_ref[rs, :] = _layer_norm(alpha * x + gate * proj, g_ref[...], b_ref[...])


def _ple(x, p, w_gate, w_proj, g, b, *, alpha, tm):
    t, d = x.shape
    pd = p.shape[1]
    row = lambda i: (i, 0)
    const2 = lambda i: (0, 0)
    return pl.pallas_call(
        functools.partial(_ple_kernel, alpha=alpha),
        grid=(t // tm,),
        in_specs=[
            pl.BlockSpec((tm, d), row),
            pl.BlockSpec((tm, pd), row),
            pl.BlockSpec(w_gate.shape, const2),
            pl.BlockSpec(w_proj.shape, const2),
            pl.BlockSpec((1, d), const2),
            pl.BlockSpec((1, d), const2),
        ],
        out_specs=pl.BlockSpec((tm, d), row),
        out_shape=jax.ShapeDtypeStruct((t, d), _F32),
        compiler_params=_params(("parallel",)),
        name="ple",
    )(x, p, w_gate, w_proj, g, b)


def _rotate_half_cols(w):
    half = w.shape[-1] // 2
    return jnp.concatenate([-w[..., half:], w[..., :half]], axis=-1)


def _rope_tables(pos, reps):
    inv = 1.0 / (ROPE_THETA ** (jnp.arange(0, QK_ROPE, 2, dtype=_F32) / QK_ROPE))
    ang = pos.astype(_F32)[:, None] * inv[None, :]
    ang = jnp.concatenate([ang, ang], -1)
    cos, sin = jnp.cos(ang), jnp.sin(ang)
    tile = lambda a: jnp.tile(a, (reps, 1))
    return (tile(jnp.concatenate([cos, sin], -1)), tile(jnp.concatenate([cos, cos], -1)),
            tile(jnp.concatenate([sin, sin], -1)))


def _tile_rows(total, preferred):
    tile = min(total, preferred)
    assert total % tile == 0
    return tile


def _prep_layer_weights(w):
    q_rank = w['g_q'].shape[0]
    kv_rank = w['g_kv'].shape[0]
    w_in = w['w_in']
    o_kr = q_rank + kv_rank
    w_kr = w_in[:, o_kr:o_kr + QK_ROPE]
    w_in_all = jnp.concatenate(
        [w_in[:, :o_kr], w_kr, _rotate_half_cols(w_kr), w_in[:, o_kr + QK_ROPE:]], axis=1)
    w_q = jnp.concatenate(
        [w['w_uq_nope'].reshape(q_rank, -1), w['w_uq_rope'].reshape(q_rank, -1),
         _rotate_half_cols(w['w_uq_rope']).reshape(q_rank, -1)], axis=1)
    row = lambda v: v.reshape(1, -1)
    return dict(
        ffn1_gu=w['ffn1_gu'].astype(_BF16), ffn1_down=w['ffn1_down'].astype(_BF16),
        ffn2_gu=w['ffn2_gu'].astype(_BF16), ffn2_down=w['ffn2_down'].astype(_BF16),
        w_in=w_in_all.astype(_BF16), w_q=w_q.astype(_BF16),
        w_ukt=jnp.transpose(w['w_uk'], (1, 2, 0)).astype(_BF16),
        w_uk=w['w_uk'].reshape(kv_rank, -1).astype(_BF16),
        w_uv=jnp.transpose(w['w_uv'], (1, 0, 2)).astype(_BF16),
        w_pool=w['w_pool'].astype(_BF16), pool_scale=row(w['pool_scale']),
        w_o=w['w_o'].astype(_BF16),
        w_ple_gate=w['w_ple_gate'].astype(_BF16), w_ple_proj=w['w_ple_proj'].astype(_BF16),
        g_q=row(w['g_q']), g_kv=row(w['g_kv']),
        ln1=(row(w['ln1_g']), row(w['ln1_b'])), ln2=(row(w['ln2_g']), row(w['ln2_b'])),
        ln3=(row(w['ln3_g']), row(w['ln3_b'])), ln4=(row(w['ln4_g']), row(w['ln4_b'])),
    )


def _encoder_layer(x, p, past, pool_hist, offset, pw, *, alpha):
    bsz, seq, d = x.shape
    t = bsz * seq
    prompt = past is None
    xf = x.reshape(t, d)
    tm = _tile_rows(t, 512)
    d_ff = pw['ffn1_down'].shape[0]
    tf = 512 if d_ff % 512 == 0 else d_ff

    x1 = _ffn_ln(xf, pw['ffn1_gu'], pw['ffn1_down'], *pw['ln1'], alpha=alpha, tm=tm, tf=tf)

    tmi = _tile_rows(t, 256)
    if prompt:
        assert seq % tmi == 0
        tabs = _rope_tables(offset + jnp.arange(seq), 1)
    else:
        assert tmi % seq == 0
        tabs = _rope_tables(offset + jnp.arange(seq), tmi // seq)
    w_uk = pw['w_uk'] if prompt else pw['w_ukt']
    proj = _inproj(x1, pw['w_in'], pw['g_q'], pw['g_kv'], pw['w_q'], w_uk, *tabs, tm=tmi,
                   absorbed=not prompt)
    if prompt:
        qcat, kcat, ckv, kr, ckvb, u = proj
        tq = tk = min(seq, 512)
        olat = _attn_prompt(qcat, kcat, ckvb, batch=bsz, seq=seq, tq=tq, tk=tk)
        halo_src = u
    else:
        qlat, qrope, ckv, kr, ckvb, krb, u = proj
        olat = _attn_sample(qlat, qrope, past[0], past[1], ckvb, krb, seq=seq)
        halo_src = jnp.pad(pool_hist, ((0, 0), (HALO - POOL_HIST, 0), (0, 0))).reshape(bsz * HALO, -1)

    tmo = _tile_rows(t, 512 if prompt else 256)
    if prompt:
        assert seq % tmo == 0 and tmo % HALO == 0
    else:
        assert tmo % seq == 0
    x2 = _outproj(olat, u, halo_src, x1, pw['w_uv'], pw['w_pool'], pw['pool_scale'], pw['w_o'],
                  *pw['ln2'], alpha=alpha, seq=seq, offset=offset, tm=tmo, prompt=prompt)
    x3 = _ffn_ln(x2, pw['ffn2_gu'], pw['ffn2_down'], *pw['ln3'], alpha=alpha, tm=tm, tf=tf)
    y = _ple(x3, p.reshape(t, -1), pw['w_ple_gate'], pw['w_ple_proj'], *pw['ln4'],
             alpha=alpha, tm=tm)

    u3 = u.reshape(bsz, seq, -1)
    if seq >= POOL_HIST:
        new_hist = u3[:, seq - POOL_HIST:]
    else:
        new_hist = jnp.concatenate([pool_hist, u3], axis=1)[:, -POOL_HIST:]
    return (y.reshape(bsz, seq, d), ckv.reshape(bsz, seq, -1), kr.reshape(bsz, seq, -1), new_hist)


def kernel(x_prompt, x_sample, cache_kv_latent, cache_k_rope, state_pool, p_prompt, p_sample, ffn1_gu, ffn1_down, ln1_g, ln1_b, w_in, g_q, g_kv, w_uq_nope, w_uq_rope, w_uk, w_uv, w_pool, pool_scale, w_o, ln2_g, ln2_b, ffn2_gu, ffn2_down, ln3_g, ln3_b, w_ple_gate, w_ple_proj, ln4_g, ln4_b):
    depth = ffn1_gu.shape[0]
    past_len = cache_kv_latent.shape[2]
    alpha = (2.0 * depth) ** 0.25
    stacked = dict(
        ffn1_gu=ffn1_gu, ffn1_down=ffn1_down, ln1_g=ln1_g, ln1_b=ln1_b, w_in=w_in, g_q=g_q,
        g_kv=g_kv, w_uq_nope=w_uq_nope, w_uq_rope=w_uq_rope, w_uk=w_uk, w_uv=w_uv,
        w_pool=w_pool, pool_scale=pool_scale, w_o=w_o, ln2_g=ln2_g, ln2_b=ln2_b,
        ffn2_gu=ffn2_gu, ffn2_down=ffn2_down, ln3_g=ln3_g, ln3_b=ln3_b,
        w_ple_gate=w_ple_gate, w_ple_proj=w_ple_proj, ln4_g=ln4_g, ln4_b=ln4_b)
    hp, hs = x_prompt, x_sample
    outs = [[] for _ in range(6)]
    for i in range(depth):
        pw = _prep_layer_weights({k: v[i] for k, v in stacked.items()})
        hp, c1, k1, s1 = _encoder_layer(hp, p_prompt[i], None, None, 0, pw, alpha=alpha)
        hs, c2, k2, s2 = _encoder_layer(
            hs, p_sample[i], (cache_kv_latent[i], cache_k_rope[i]), state_pool[i], past_len, pw,
            alpha=alpha)
        for acc, v in zip(outs, (c1, k1, s1, c2, k2, s2)):
            acc.append(v)
    return (hp, hs) + tuple(jnp.stack(v) for v in outs)
```

```python
import functools
import math

import jax
import jax.numpy as jnp
from jax import lax
from jax.experimental import pallas as pl
from jax.experimental.pallas import tpu as pltpu

CHUNK = 64
N_HEADS = 8
QK_NOPE = 128
QK_ROPE = 64
V_HEAD = 128
POOL_WINDOWS = (2, 4, 8, 16)
POOL_HIST = max(POOL_WINDOWS) - 1
HALO = POOL_HIST + 1
ROPE_THETA = 10000.0
LN_EPS = 1e-5
RMS_EPS = 1e-6
ATTN_SCALE = (QK_NOPE + QK_ROPE) ** -0.5
SCORE_SCALE = ATTN_SCALE * math.log2(math.e)

V7X_VMEM_LIMIT_BYTES = 56 * 1024 * 1024
STAT_LANES = 128

_F32 = jnp.float32
_BF16 = jnp.bfloat16
_NT_DIMS = (((1,), (1,)), ((), ()))


def _dot(a, b):
    return jnp.dot(a, b, preferred_element_type=_F32)


def _dot_nt(a, b):
    return lax.dot_general(a, b, _NT_DIMS, preferred_element_type=_F32)


def _layer_norm(y, g, b):
    mu = jnp.mean(y, axis=-1, keepdims=True)
    d = y - mu
    var = jnp.mean(d * d, axis=-1, keepdims=True)
    return d * lax.rsqrt(var + LN_EPS) * g + b


def _rms_norm(z, g):
    return z * lax.rsqrt(jnp.mean(z * z, axis=-1, keepdims=True) + RMS_EPS) * g


def _row_groups(rows, group_rows=256):
    group_rows = min(rows, group_rows)
    return [slice(r, r + group_rows) for r in range(0, rows, group_rows)]


def _params(semantics):
    return pltpu.CompilerParams(
        dimension_semantics=semantics, vmem_limit_bytes=V7X_VMEM_LIMIT_BYTES)


def _ffn_ln_kernel(x_ref, wg_ref, wu_ref, wd_ref, g_ref, b_ref, o_ref, xb_ref, *, alpha):
    j = pl.program_id(1)

    @pl.when(j == 0)
    def _():
        xb_ref[...] = x_ref[...].astype(_BF16)
        o_ref[...] = jnp.zeros_like(o_ref)

    xb = xb_ref[...]
    gate = _dot(xb, wg_ref[...])
    up = _dot(xb, wu_ref[...])
    act = (gate * jax.nn.sigmoid(gate) * up).astype(_BF16)
    o_ref[...] += _dot(act, wd_ref[...])

    @pl.when(j == pl.num_programs(1) - 1)
    def _():
        y = alpha * x_ref[...] + 0.5 * o_ref[...]
        o_ref[...] = _layer_norm(y, g_ref[...], b_ref[...])


def _ffn_ln(x, w_gu, w_down, g, b, *, alpha, tm, tf):
    t, d = x.shape
    f = w_down.shape[0]
    nj = f // tf
    return pl.pallas_call(
        functools.partial(_ffn_ln_kernel, alpha=alpha),
        grid=(t // tm, nj),
        in_specs=[
            pl.BlockSpec((tm, d), lambda i, j: (i, 0)),
            pl.BlockSpec((d, tf), lambda i, j: (0, j)),
            pl.BlockSpec((d, tf), lambda i, j: (0, j + nj)),
            pl.BlockSpec((tf, d), lambda i, j: (j, 0)),
            pl.BlockSpec((1, d), lambda i, j: (0, 0)),
            pl.BlockSpec((1, d), lambda i, j: (0, 0)),
        ],
        out_specs=pl.BlockSpec((tm, d), lambda i, j: (i, 0)),
        out_shape=jax.ShapeDtypeStruct((t, d), _F32),
        scratch_shapes=[pltpu.VMEM((tm, d), _BF16)],
        compiler_params=_params(("parallel", "arbitrary")),
        name="ffn_ln",
    )(x, w_gu, w_gu, w_down, g, b)


def _inproj_kernel(x_ref, win_ref, gq_ref, gkv_ref, wq_ref, wuk_ref, cs_ref, cc_ref, ss_ref,
                   *out_refs, q_rank, kv_rank, absorbed):
    if absorbed:
        qlat_ref, qrope_ref, ckv_ref, kr_ref, ckvb_ref, krb_ref, u_ref = out_refs
    else:
        qcat_ref, kcat_ref, ckv_ref, kr_ref, ckvb_ref, u_ref = out_refs
    xb = x_ref[...].astype(_BF16)
    z = _dot(xb, win_ref[...])
    o_kv = q_rank
    o_kr = o_kv + kv_rank
    o_u = o_kr + 2 * QK_ROPE
    u_ref[...] = z[:, o_u:]

    c_kv = _rms_norm(z[:, o_kv:o_kr], gkv_ref[...])
    c_kv_b = c_kv.astype(_BF16)
    ckv_ref[...] = c_kv
    ckvb_ref[...] = c_kv_b

    kt = z[:, o_kr:o_u] * cs_ref[...]
    k2 = kt + pltpu.roll(kt, QK_ROPE, axis=1)
    kr_ref[...] = k2[:, :QK_ROPE]

    c_q = _rms_norm(z[:, :q_rank], gq_ref[...]).astype(_BF16)
    qz = _dot(c_q, wq_ref[...])
    n_nope = N_HEADS * QK_NOPE
    n_rope = N_HEADS * QK_ROPE
    lane_reps = n_rope // cc_ref.shape[1]
    cos = jnp.concatenate([cc_ref[...]] * lane_reps, axis=1)
    sin = jnp.concatenate([ss_ref[...]] * lane_reps, axis=1)
    q_rope = (qz[:, n_nope:n_nope + n_rope] * cos + qz[:, n_nope + n_rope:] * sin) * SCORE_SCALE
    if absorbed:
        krb_ref[...] = k2[:, :QK_ROPE].astype(_BF16)
        q_rope = q_rope.astype(_BF16)
        q_nope = qz[:, :n_nope].astype(_BF16)
        for h in range(N_HEADS):
            q_lat = _dot(q_nope[:, h * QK_NOPE:(h + 1) * QK_NOPE], wuk_ref[h]) * SCORE_SCALE
            qlat_ref[h] = q_lat.astype(_BF16)
            qrope_ref[h] = q_rope[:, h * QK_ROPE:(h + 1) * QK_ROPE]
    else:
        pair = 2 * QK_ROPE
        low = lax.broadcasted_iota(jnp.int32, (x_ref.shape[0], pair), 1) < QK_ROPE
        k_tail = jnp.where(low, k2, 0.0).astype(_BF16)
        q_nope = (qz[:, :n_nope] * SCORE_SCALE).astype(_BF16)
        k_nope = _dot(c_kv_b, wuk_ref[...]).astype(_BF16)
        for h in range(N_HEADS):
            v = q_rope[:, (h // 2) * pair:(h // 2 + 1) * pair]
            if h % 2:
                v = pltpu.roll(v, QK_ROPE, axis=1)
            q_tail = jnp.where(low, v, 0.0).astype(_BF16)
            hs = slice(h * QK_NOPE, (h + 1) * QK_NOPE)
            qcat_ref[h] = jnp.concatenate([q_nope[:, hs], q_tail], axis=1)
            kcat_ref[h] = jnp.concatenate([k_nope[:, hs], k_tail], axis=1)


def _inproj(x, w_in, g_q, g_kv, w_q, w_uk, cs, cc, ss, *, tm, absorbed):
    t, d = x.shape
    q_rank = g_q.shape[1]
    kv_rank = g_kv.shape[1]
    pool_width = w_in.shape[1] - q_rank - kv_rank - 2 * QK_ROPE
    nper = cs.shape[0] // tm
    const2 = lambda i: (0, 0)
    row = lambda i: (i, 0)
    tab = lambda i: (i % nper, 0)
    head_row = lambda i: (0, i, 0)

    def rows_out(width, dtype):
        return pl.BlockSpec((tm, width), row), jax.ShapeDtypeStruct((t, width), dtype)

    def heads_out(width):
        return (pl.BlockSpec((N_HEADS, tm, width), head_row),
                jax.ShapeDtypeStruct((N_HEADS, t, width), _BF16))

    cat = QK_NOPE + 2 * QK_ROPE
    if absorbed:
        outs = [heads_out(kv_rank), heads_out(QK_ROPE)]
    else:
        outs = [heads_out(cat), heads_out(cat)]
    outs += [rows_out(kv_rank, _F32), rows_out(QK_ROPE, _F32), rows_out(kv_rank, _BF16)]
    if absorbed:
        outs.append(rows_out(QK_ROPE, _BF16))
    outs.append(rows_out(pool_width, _F32))
    return pl.pallas_call(
        functools.partial(_inproj_kernel, q_rank=q_rank, kv_rank=kv_rank, absorbed=absorbed),
        grid=(t // tm,),
        in_specs=[
            pl.BlockSpec((tm, d), row),
            pl.BlockSpec(w_in.shape, const2),
            pl.BlockSpec((1, q_rank), const2),
            pl.BlockSpec((1, kv_rank), const2),
            pl.BlockSpec(w_q.shape, const2),
            pl.BlockSpec(w_uk.shape, lambda i: (0,) * w_uk.ndim),
            pl.BlockSpec((tm, cs.shape[1]), tab),
            pl.BlockSpec((tm, cc.shape[1]), tab),
            pl.BlockSpec((tm, ss.shape[1]), tab),
        ],
        out_specs=[o[0] for o in outs],
        out_shape=[o[1] for o in outs],
        compiler_params=_params(("parallel",)),
        name="inproj",
    )(x, w_in, g_q, g_kv, w_q, w_uk, cs, cc, ss)


def _attn_prompt_kernel(q_ref, k_ref, v_ref, o_ref, s0_ref, s1_ref, p0_ref, p1_ref, a0_ref, a1_ref,
                        m_ref, l_ref, acc_ref, *, tq, tk):
    i = pl.program_id(2)
    heads = q_ref.shape[0]
    m_ref[...] = jnp.full_like(m_ref, -jnp.inf)
    l_ref[...] = jnp.zeros_like(l_ref)
    acc_ref[...] = jnp.zeros_like(acc_ref)
    s_bufs = (s0_ref, s1_ref)
    p_bufs = ((p0_ref, a0_ref), (p1_ref, a1_ref))

    def keys(j):
        return pl.ds(pl.multiple_of(j * tk, tk), tk)

    def lanes(v, width):
        return jnp.concatenate([v] * (width // v.shape[1]), axis=1)

    def row_stat(v):
        return jnp.broadcast_to(v, (v.shape[0], STAT_LANES))

    def scores(j, b):
        for h in range(heads):
            s_bufs[b][h] = _dot_nt(q_ref[h], k_ref[h, keys(j), :])

    def softmax(j, b, masked=False):
        p_ref, a_ref = p_bufs[b]
        for h in range(heads):
            s = s_bufs[b][h]
            if masked:
                q_tok = lax.broadcasted_iota(jnp.int32, (tq, 1), 0) + i * tq
                k_tok = lax.broadcasted_iota(jnp.int32, (1, tk), 1) + j * tk
                s = jnp.where(k_tok // CHUNK <= q_tok // CHUNK, s, -jnp.inf)
            m_old = m_ref[h]
            m_new = jnp.maximum(m_old, row_stat(jnp.max(s, axis=1, keepdims=True)))
            p = jnp.exp2(s - lanes(m_new, tk))
            a = jnp.exp2(m_old - m_new)
            l_ref[h] = a * l_ref[h] + row_stat(jnp.sum(p, axis=1, keepdims=True))
            m_ref[h] = m_new
            a_ref[h] = a
            p_ref[h] = pltpu.bitcast(p.astype(_BF16), jnp.uint32)

    def values(j, b):
        p_ref, a_ref = p_bufs[b]
        for h in range(heads):
            acc_ref[h] = (lanes(a_ref[h], acc_ref.shape[2]) * acc_ref[h]
                          + _dot(pltpu.bitcast(p_ref[h], _BF16), v_ref[keys(j), :]))

    n_full = (i * tq) // tk
    scores(0, 0)

    @pl.when(n_full == 0)
    def _():
        softmax(0, 0, masked=True)
        values(0, 0)

    @pl.when(n_full > 0)
    def _():
        scores(1, 1)
        softmax(0, 0)

        def pair(k, carry):
            t = 2 * k + 2
            scores(t, 0)
            softmax(t - 1, 1)
            values(t - 2, 0)
            scores(t + 1, 1)
            softmax(t, 0)
            values(t - 1, 1)
            return carry

        lax.fori_loop(0, (n_full - 1) // 2, pair, 0)

        @pl.when(n_full % 2 == 1)
        def _():
            softmax(n_full, 1, masked=True)
            values(n_full - 1, 0)
            values(n_full, 1)

        @pl.when(n_full % 2 == 0)
        def _():
            scores(n_full, 0)
            softmax(n_full - 1, 1)
            values(n_full - 2, 0)
            softmax(n_full, 0, masked=True)
            values(n_full - 1, 1)
            values(n_full, 0)

    for h in range(heads):
        o_ref[h] = (acc_ref[h] / lanes(l_ref[h], acc_ref.shape[2])).astype(o_ref.dtype)


def _attn_prompt(qcat, kcat, ckvb, *, batch, seq, tq, tk, heads):
    assert tk % tq == 0 and tq % CHUNK == 0 and seq % tk == 0
    h, t, e = qcat.shape
    c = ckvb.shape[1]
    nq = seq // tq
    q_block = lambda b, hg, i: (hg, b * nq + i, 0)
    stat = pltpu.VMEM((heads, tq, STAT_LANES), _F32)
    tile = pltpu.VMEM((heads, tq, tk), _F32)
    packed = pltpu.VMEM((heads, tq // 2, tk), jnp.uint32)
    return pl.pallas_call(
        functools.partial(_attn_prompt_kernel, tq=tq, tk=tk),
        grid=(batch, h // heads, nq),
        in_specs=[
            pl.BlockSpec((heads, tq, e), q_block),
            pl.BlockSpec((heads, seq, e), lambda b, hg, i: (hg, b, 0)),
            pl.BlockSpec((seq, c), lambda b, hg, i: (b, 0)),
        ],
        out_specs=pl.BlockSpec((heads, tq, c), q_block),
        out_shape=jax.ShapeDtypeStruct((h, t, c), _BF16),
        scratch_shapes=[
            tile, tile,
            packed, packed,
            stat, stat,
            stat,
            stat,
            pltpu.VMEM((heads, tq, c), _F32),
        ],
        compiler_params=_params(("parallel", "parallel", "arbitrary")),
        name="attn_prompt",
    )(qcat, kcat, ckvb)


def _attn_sample_kernel(ql_ref, qr_ref, pkv_ref, pkr_ref, kv_ref, kr_ref, o_ref):
    rows = ql_ref.shape[0] * ql_ref.shape[1]
    ql = ql_ref[...].reshape(rows, ql_ref.shape[2])
    qr = qr_ref[...].reshape(rows, qr_ref.shape[2])
    pkv = pkv_ref[0].astype(_BF16)
    pkr = pkr_ref[0].astype(_BF16)
    kv = kv_ref[...]
    kr = kr_ref[...]
    s_past = _dot_nt(ql, pkv) + _dot_nt(qr, pkr)
    s_new = _dot_nt(ql, kv) + _dot_nt(qr, kr)
    m = jnp.maximum(jnp.max(s_past, axis=1, keepdims=True), jnp.max(s_new, axis=1, keepdims=True))
    p_past = jnp.exp2(s_past - m)
    p_new = jnp.exp2(s_new - m)
    l = jnp.sum(p_past, axis=1, keepdims=True) + jnp.sum(p_new, axis=1, keepdims=True)
    o = (_dot(p_past.astype(_BF16), pkv) + _dot(p_new.astype(_BF16), kv)) / l
    o_ref[...] = o.reshape(o_ref.shape).astype(o_ref.dtype)


def _attn_sample(qlat, qrope, past_kv, past_kr, ckvb, krb, *, seq):
    h, t, c = qlat.shape
    r = qrope.shape[2]
    batch, past_len, _ = past_kv.shape
    head_row = lambda b: (0, b, 0)
    return pl.pallas_call(
        _attn_sample_kernel,
        grid=(batch,),
        in_specs=[
            pl.BlockSpec((h, seq, c), head_row),
            pl.BlockSpec((h, seq, r), head_row),
            pl.BlockSpec((1, past_len, c), lambda b: (b, 0, 0)),
            pl.BlockSpec((1, past_len, r), lambda b: (b, 0, 0)),
            pl.BlockSpec((seq, c), lambda b: (b, 0)),
            pl.BlockSpec((seq, r), lambda b: (b, 0)),
        ],
        out_specs=pl.BlockSpec((h, seq, c), head_row),
        out_shape=jax.ShapeDtypeStruct((h, t, c), _BF16),
        compiler_params=_params(("parallel",)),
        name="attn_sample",
    )(qlat, qrope, past_kv, past_kr, ckvb, krb)


def _pool_delta(ext, pos0, group):
    n = ext.shape[0] - HALO
    pos = pos0 + lax.broadcasted_iota(jnp.int32, (n, 1), 0)
    out = []
    for gi, w in enumerate(POOL_WINDOWS):
        cols = ext[:, gi * group:(gi + 1) * group]
        s = cols
        shift = 1
        while shift < w:
            s = s + pltpu.roll(s, shift, axis=0)
            shift *= 2
        cnt = jnp.minimum(w, pos + 1).astype(_F32)
        out.append(s[HALO:] / cnt - cols[HALO:])
    return out


def _outproj_kernel(olat_ref, u_ref, halo_ref, x_ref, wuv_ref, wpool_ref, pscale_ref, wo_ref,
                    g_ref, b_ref, o_ref, *, alpha, seq, offset, blocks_per_seq):
    tm = u_ref.shape[0]
    group = u_ref.shape[1] // len(POOL_WINDOWS)
    u = u_ref[...]
    halo = halo_ref[...]
    if blocks_per_seq is not None:
        bi = pl.program_id(0) % blocks_per_seq
        halo = jnp.where(bi == 0, 0.0, halo)
        deltas = [_pool_delta(jnp.concatenate([halo, u], axis=0), offset + bi * tm, group)]
    else:
        deltas = []
        for s in range(tm // seq):
            ext = jnp.concatenate(
                [halo[s * HALO:(s + 1) * HALO], u[s * seq:(s + 1) * seq]], axis=0)
            deltas.append(_pool_delta(ext, offset, group))
    pooled = []
    for gi in range(len(POOL_WINDOWS)):
        d = jnp.concatenate([ds[gi] for ds in deltas], axis=0) if len(deltas) > 1 else deltas[0][gi]
        pooled.append(_dot(d.astype(_BF16), wpool_ref[gi]))
    o_pool = jnp.concatenate(pooled, axis=1) * pscale_ref[...]
    o_attn = jnp.concatenate(
        [_dot(olat_ref[h], wuv_ref[h]) for h in range(N_HEADS)], axis=1)
    mixed = jnp.concatenate([o_attn, o_pool], axis=1).astype(_BF16)
    for rs in _row_groups(tm):
        y = alpha * x_ref[rs, :] + _dot(mixed[rs], wo_ref[...])
        o_ref[rs, :] = _layer_norm(y, g_ref[...], b_ref[...])


def _outproj(olat, u, halo_src, x, w_uv, w_pool, pool_scale, w_o, g, b,
             *, alpha, seq, offset, tm, prompt):
    t, d = x.shape
    h, _, c = olat.shape
    pw = u.shape[1]
    row = lambda i: (i, 0)
    const2 = lambda i: (0, 0)
    const3 = lambda i: (0, 0, 0)
    if prompt:
        blocks_per_seq = seq // tm
        step = tm // HALO
        halo_spec = pl.BlockSpec((HALO, pw), lambda i: (jnp.maximum(i * step - 1, 0), 0))
    else:
        blocks_per_seq = None
        halo_spec = pl.BlockSpec(((tm // seq) * HALO, pw), row)
    return pl.pallas_call(
        functools.partial(_outproj_kernel, alpha=alpha, seq=seq, offset=offset,
                          blocks_per_seq=blocks_per_seq),
        grid=(t // tm,),
        in_specs=[
            pl.BlockSpec((h, tm, c), lambda i: (0, i, 0)),
            pl.BlockSpec((tm, pw), row),
            halo_spec,
            pl.BlockSpec((tm, d), row),
            pl.BlockSpec(w_uv.shape, const3),
            pl.BlockSpec(w_pool.shape, const3),
            pl.BlockSpec((1, pw), const2),
            pl.BlockSpec(w_o.shape, const2),
            pl.BlockSpec((1, d), const2),
            pl.BlockSpec((1, d), const2),
        ],
        out_specs=pl.BlockSpec((tm, d), row),
        out_shape=jax.ShapeDtypeStruct((t, d), _F32),
        compiler_params=_params(("parallel",)),
        name="outproj_prompt" if prompt else "outproj_sample",
    )(olat, u, halo_src, x, w_uv, w_pool, pool_scale, w_o, g, b)


def _ple_kernel(x_ref, p_ref, wg_ref, wp_ref, g_ref, b_ref, o_ref, *, alpha):
    for rs in _row_groups(x_ref.shape[0]):
        x = x_ref[rs, :]
        gate = jax.nn.sigmoid(_dot(x.astype(_BF16), wg_ref[...]))
        proj = _dot(p_ref[rs, :].astype(_BF16), wp_ref[...])
        o_ref[rs, :] = _layer_norm(alpha * x + gate * proj, g_ref[...], b_ref[...])


def _ple(x, p, w_gate, w_proj, g, b, *, alpha, tm):
    t, d = x.shape
    pd = p.shape[1]
    row = lambda i: (i, 0)
    const2 = lambda i: (0, 0)
    return pl.pallas_call(
        functools.partial(_ple_kernel, alpha=alpha),
        grid=(t // tm,),
        in_specs=[
            pl.BlockSpec((tm, d), row),
            pl.BlockSpec((tm, pd), row),
            pl.BlockSpec(w_gate.shape, const2),
            pl.BlockSpec(w_proj.shape, const2),
            pl.BlockSpec((1, d), const2),
            pl.BlockSpec((1, d), const2),
        ],
        out_specs=pl.BlockSpec((tm, d), row),
        out_shape=jax.ShapeDtypeStruct((t, d), _F32),
        compiler_params=_params(("parallel",)),
        name="ple",
    )(x, p, w_gate, w_proj, g, b)


def _rotate_half_cols(w):
    half = w.shape[-1] // 2
    return jnp.concatenate([-w[..., half:], w[..., :half]], axis=-1)


def _rope_tables(pos, reps):
    inv = 1.0 / (ROPE_THETA ** (jnp.arange(0, QK_ROPE, 2, dtype=_F32) / QK_ROPE))
    ang = pos.astype(_F32)[:, None] * inv[None, :]
    ang = jnp.concatenate([ang, ang], -1)
    cos, sin = jnp.cos(ang), jnp.sin(ang)
    tile = lambda a: jnp.tile(a, (reps, 1))
    return (tile(jnp.concatenate([cos, sin], -1)), tile(jnp.concatenate([cos, cos], -1)),
            tile(jnp.concatenate([sin, sin], -1)))


def _tile_rows(total, preferred):
    tile = min(total, preferred)
    assert total % tile == 0
    return tile


def _prep_layer_weights(w):
    q_rank = w['g_q'].shape[0]
    kv_rank = w['g_kv'].shape[0]
    w_in = w['w_in']
    o_kr = q_rank + kv_rank
    w_kr = w_in[:, o_kr:o_kr + QK_ROPE]
    w_in_all = jnp.concatenate(
        [w_in[:, :o_kr], w_kr, _rotate_half_cols(w_kr), w_in[:, o_kr + QK_ROPE:]], axis=1)
    w_q = jnp.concatenate(
        [w['w_uq_nope'].reshape(q_rank, -1), w['w_uq_rope'].reshape(q_rank, -1),
         _rotate_half_cols(w['w_uq_rope']).reshape(q_rank, -1)], axis=1)
    row = lambda v: v.reshape(1, -1)
    return dict(
        ffn1_gu=w['ffn1_gu'].astype(_BF16), ffn1_down=w['ffn1_down'].astype(_BF16),
        ffn2_gu=w['ffn2_gu'].astype(_BF16), ffn2_down=w['ffn2_down'].astype(_BF16),
        w_in=w_in_all.astype(_BF16), w_q=w_q.astype(_BF16),
        w_ukt=jnp.transpose(w['w_uk'], (1, 2, 0)).astype(_BF16),
        w_uk=w['w_uk'].reshape(kv_rank, -1).astype(_BF16),
        w_uv=jnp.transpose(w['w_uv'], (1, 0, 2)).astype(_BF16),
        w_pool=w['w_pool'].astype(_BF16), pool_scale=row(w['pool_scale']),
        w_o=w['w_o'].astype(_BF16),
        w_ple_gate=w['w_ple_gate'].astype(_BF16), w_ple_proj=w['w_ple_proj'].astype(_BF16),
        g_q=row(w['g_q']), g_kv=row(w['g_kv']),
        ln1=(row(w['ln1_g']), row(w['ln1_b'])), ln2=(row(w['ln2_g']), row(w['ln2_b'])),
        ln3=(row(w['ln3_g']), row(w['ln3_b'])), ln4=(row(w['ln4_g']), row(w['ln4_b'])),
    )


def _encoder_layer(x, p, past, pool_hist, offset, pw, *, alpha):
    bsz, seq, d = x.shape
    t = bsz * seq
    prompt = past is None
    xf = x.reshape(t, d)
    tm = _tile_rows(t, 512)
    d_ff = pw['ffn1_down'].shape[0]
    tf = 512 if d_ff % 512 == 0 else d_ff

    x1 = _ffn_ln(xf, pw['ffn1_gu'], pw['ffn1_down'], *pw['ln1'], alpha=alpha, tm=tm, tf=tf)

    tmi = _tile_rows(t, 256)
    if prompt:
        assert seq % tmi == 0
        tabs = _rope_tables(offset + jnp.arange(seq), 1)
    else:
        assert tmi % seq == 0
        tabs = _rope_tables(offset + jnp.arange(seq), tmi // seq)
    w_uk = pw['w_uk'] if prompt else pw['w_ukt']
    proj = _inproj(x1, pw['w_in'], pw['g_q'], pw['g_kv'], pw['w_q'], w_uk, *tabs, tm=tmi,
                   absorbed=not prompt)
    if prompt:
        qcat, kcat, ckv, kr, ckvb, u = proj
        tq = tk = min(seq, 512)
        olat = _attn_prompt(qcat, kcat, ckvb, batch=bsz, seq=seq, tq=tq, tk=tk, heads=2)
        halo_src = u
    else:
        qlat, qrope, ckv, kr, ckvb, krb, u = proj
        olat = _attn_sample(qlat, qrope, past[0], past[1], ckvb, krb, seq=seq)
        halo_src = jnp.pad(pool_hist, ((0, 0), (HALO - POOL_HIST, 0), (0, 0))).reshape(bsz * HALO, -1)

    tmo = _tile_rows(t, 512 if prompt else 256)
    if prompt:
        assert seq % tmo == 0 and tmo % HALO == 0
    else:
        assert tmo % seq == 0
    x2 = _outproj(olat, u, halo_src, x1, pw['w_uv'], pw['w_pool'], pw['pool_scale'], pw['w_o'],
                  *pw['ln2'], alpha=alpha, seq=seq, offset=offset, tm=tmo, prompt=prompt)
    x3 = _ffn_ln(x2, pw['ffn2_gu'], pw['ffn2_down'], *pw['ln3'], alpha=alpha, tm=tm, tf=tf)
    y = _ple(x3, p.reshape(t, -1), pw['w_ple_gate'], pw['w_ple_proj'], *pw['ln4'],
             alpha=alpha, tm=tm)

    u3 = u.reshape(bsz, seq, -1)
    if seq >= POOL_HIST:
        new_hist = u3[:, seq - POOL_HIST:]
    else:
        new_hist = jnp.concatenate([pool_hist, u3], axis=1)[:, -POOL_HIST:]
    return (y.reshape(bsz, seq, d), ckv.reshape(bsz, seq, -1), kr.reshape(bsz, seq, -1), new_hist)


def kernel(x_prompt, x_sample, cache_kv_latent, cache_k_rope, state_pool, p_prompt, p_sample, ffn1_gu, ffn1_down, ln1_g, ln1_b, w_in, g_q, g_kv, w_uq_nope, w_uq_rope, w_uk, w_uv, w_pool, pool_scale, w_o, ln2_g, ln2_b, ffn2_gu, ffn2_down, ln3_g, ln3_b, w_ple_gate, w_ple_proj, ln4_g, ln4_b):
    depth = ffn1_gu.shape[0]
    past_len = cache_kv_latent.shape[2]
    alpha = (2.0 * depth) ** 0.25
    stacked = dict(
        ffn1_gu=ffn1_gu, ffn1_down=ffn1_down, ln1_g=ln1_g, ln1_b=ln1_b, w_in=w_in, g_q=g_q,
        g_kv=g_kv, w_uq_nope=w_uq_nope, w_uq_rope=w_uq_rope, w_uk=w_uk, w_uv=w_uv,
        w_pool=w_pool, pool_scale=pool_scale, w_o=w_o, ln2_g=ln2_g, ln2_b=ln2_b,
        ffn2_gu=ffn2_gu, ffn2_down=ffn2_down, ln3_g=ln3_g, ln3_b=ln3_b,
        w_ple_gate=w_ple_gate, w_ple_proj=w_ple_proj, ln4_g=ln4_g, ln4_b=ln4_b)
    hp, hs = x_prompt, x_sample
    outs = [[] for _ in range(6)]
    for i in range(depth):
        pw = _prep_layer_weights({k: v[i] for k, v in stacked.items()})
        hp, c1, k1, s1 = _encoder_layer(hp, p_prompt[i], None, None, 0, pw, alpha=alpha)
        hs, c2, k2, s2 = _encoder_layer(
            hs, p_sample[i], (cache_kv_latent[i], cache_k_rope[i]), state_pool[i], past_len, pw,
            alpha=alpha)
        for acc, v in zip(outs, (c1, k1, s1, c2, k2, s2)):
            acc.append(v)
    return (hp, hs) + tuple(jnp.stack(v) for v in outs)
```

```python
import functools
import math

import jax
import jax.numpy as jnp
from jax import lax
from jax.experimental import pallas as pl
from jax.experimental.pallas import tpu as pltpu

CHUNK = 64
N_HEADS = 8
QK_NOPE = 128
QK_ROPE = 64
V_HEAD = 128
POOL_WINDOWS = (2, 4, 8, 16)
POOL_HIST = max(POOL_WINDOWS) - 1
HALO = POOL_HIST + 1
ROPE_THETA = 10000.0
LN_EPS = 1e-5
RMS_EPS = 1e-6
ATTN_SCALE = (QK_NOPE + QK_ROPE) ** -0.5
SCORE_SCALE = ATTN_SCALE * math.log2(math.e)

V7X_VMEM_LIMIT_BYTES = 56 * 1024 * 1024
STAT_LANES = 128
SOFTMAX_ROW_PARTS = 4

_F32 = jnp.float32
_BF16 = jnp.bfloat16
_NT_DIMS = (((1,), (1,)), ((), ()))


def _dot(a, b):
    return jnp.dot(a, b, preferred_element_type=_F32)


def _dot_nt(a, b):
    return lax.dot_general(a, b, _NT_DIMS, preferred_element_type=_F32)


def _layer_norm(y, g, b):
    mu = jnp.mean(y, axis=-1, keepdims=True)
    d = y - mu
    var = jnp.mean(d * d, axis=-1, keepdims=True)
    return d * lax.rsqrt(var + LN_EPS) * g + b


def _rms_norm(z, g):
    return z * lax.rsqrt(jnp.mean(z * z, axis=-1, keepdims=True) + RMS_EPS) * g


def _row_groups(rows, group_rows=256):
    group_rows = min(rows, group_rows)
    return [slice(r, r + group_rows) for r in range(0, rows, group_rows)]


def _params(semantics):
    return pltpu.CompilerParams(
        dimension_semantics=semantics, vmem_limit_bytes=V7X_VMEM_LIMIT_BYTES)


def _ffn_ln_kernel(x_ref, wg_ref, wu_ref, wd_ref, g_ref, b_ref, o_ref, xb_ref, *, alpha):
    j = pl.program_id(1)

    @pl.when(j == 0)
    def _():
        xb_ref[...] = x_ref[...].astype(_BF16)
        o_ref[...] = jnp.zeros_like(o_ref)

    xb = xb_ref[...]
    gate = _dot(xb, wg_ref[...])
    up = _dot(xb, wu_ref[...])
    act = (gate * jax.nn.sigmoid(gate) * up).astype(_BF16)
    o_ref[...] += _dot(act, wd_ref[...])

    @pl.when(j == pl.num_programs(1) - 1)
    def _():
        y = alpha * x_ref[...] + 0.5 * o_ref[...]
        o_ref[...] = _layer_norm(y, g_ref[...], b_ref[...])


def _ffn_ln(x, w_gu, w_down, g, b, *, alpha, tm, tf):
    t, d = x.shape
    f = w_down.shape[0]
    nj = f // tf
    return pl.pallas_call(
        functools.partial(_ffn_ln_kernel, alpha=alpha),
        grid=(t // tm, nj),
        in_specs=[
            pl.BlockSpec((tm, d), lambda i, j: (i, 0)),
            pl.BlockSpec((d, tf), lambda i, j: (0, j)),
            pl.BlockSpec((d, tf), lambda i, j: (0, j + nj)),
            pl.BlockSpec((tf, d), lambda i, j: (j, 0)),
            pl.BlockSpec((1, d), lambda i, j: (0, 0)),
            pl.BlockSpec((1, d), lambda i, j: (0, 0)),
        ],
        out_specs=pl.BlockSpec((tm, d), lambda i, j: (i, 0)),
        out_shape=jax.ShapeDtypeStruct((t, d), _F32),
        scratch_shapes=[pltpu.VMEM((tm, d), _BF16)],
        compiler_params=_params(("parallel", "arbitrary")),
        name="ffn_ln",
    )(x, w_gu, w_gu, w_down, g, b)


def _inproj_kernel(x_ref, win_ref, gq_ref, gkv_ref, wq_ref, wuk_ref, cs_ref, cc_ref, ss_ref,
                   *out_refs, q_rank, kv_rank, absorbed):
    if absorbed:
        qlat_ref, qrope_ref, ckv_ref, kr_ref, ckvb_ref, krb_ref, u_ref = out_refs
    else:
        qcat_ref, kcat_ref, ckv_ref, kr_ref, ckvb_ref, u_ref = out_refs
    xb = x_ref[...].astype(_BF16)
    z = _dot(xb, win_ref[...])
    o_kv = q_rank
    o_kr = o_kv + kv_rank
    o_u = o_kr + 2 * QK_ROPE
    u_ref[...] = z[:, o_u:]

    c_kv = _rms_norm(z[:, o_kv:o_kr], gkv_ref[...])
    c_kv_b = c_kv.astype(_BF16)
    ckv_ref[...] = c_kv
    ckvb_ref[...] = c_kv_b

    kt = z[:, o_kr:o_u] * cs_ref[...]
    k2 = kt + pltpu.roll(kt, QK_ROPE, axis=1)
    kr_ref[...] = k2[:, :QK_ROPE]

    c_q = _rms_norm(z[:, :q_rank], gq_ref[...]).astype(_BF16)
    qz = _dot(c_q, wq_ref[...])
    n_nope = N_HEADS * QK_NOPE
    n_rope = N_HEADS * QK_ROPE
    lane_reps = n_rope // cc_ref.shape[1]
    cos = jnp.concatenate([cc_ref[...]] * lane_reps, axis=1)
    sin = jnp.concatenate([ss_ref[...]] * lane_reps, axis=1)
    q_rope = (qz[:, n_nope:n_nope + n_rope] * cos + qz[:, n_nope + n_rope:] * sin) * SCORE_SCALE
    if absorbed:
        krb_ref[...] = k2[:, :QK_ROPE].astype(_BF16)
        q_rope = q_rope.astype(_BF16)
        q_nope = qz[:, :n_nope].astype(_BF16)
        for h in range(N_HEADS):
            q_lat = _dot(q_nope[:, h * QK_NOPE:(h + 1) * QK_NOPE], wuk_ref[h]) * SCORE_SCALE
            qlat_ref[h] = q_lat.astype(_BF16)
            qrope_ref[h] = q_rope[:, h * QK_ROPE:(h + 1) * QK_ROPE]
    else:
        pair = 2 * QK_ROPE
        low = lax.broadcasted_iota(jnp.int32, (x_ref.shape[0], pair), 1) < QK_ROPE
        k_tail = jnp.where(low, k2, 0.0).astype(_BF16)
        q_nope = (qz[:, :n_nope] * SCORE_SCALE).astype(_BF16)
        k_nope = _dot(c_kv_b, wuk_ref[...]).astype(_BF16)
        for h in range(N_HEADS):
            v = q_rope[:, (h // 2) * pair:(h // 2 + 1) * pair]
            if h % 2:
                v = pltpu.roll(v, QK_ROPE, axis=1)
            q_tail = jnp.where(low, v, 0.0).astype(_BF16)
            hs = slice(h * QK_NOPE, (h + 1) * QK_NOPE)
            qcat_ref[h] = jnp.concatenate([q_nope[:, hs], q_tail], axis=1)
            kcat_ref[h] = jnp.concatenate([k_nope[:, hs], k_tail], axis=1)


def _inproj(x, w_in, g_q, g_kv, w_q, w_uk, cs, cc, ss, *, tm, absorbed):
    t, d = x.shape
    q_rank = g_q.shape[1]
    kv_rank = g_kv.shape[1]
    pool_width = w_in.shape[1] - q_rank - kv_rank - 2 * QK_ROPE
    nper = cs.shape[0] // tm
    const2 = lambda i: (0, 0)
    row = lambda i: (i, 0)
    tab = lambda i: (i % nper, 0)
    head_row = lambda i: (0, i, 0)

    def rows_out(width, dtype):
        return pl.BlockSpec((tm, width), row), jax.ShapeDtypeStruct((t, width), dtype)

    def heads_out(width):
        return (pl.BlockSpec((N_HEADS, tm, width), head_row),
                jax.ShapeDtypeStruct((N_HEADS, t, width), _BF16))

    cat = QK_NOPE + 2 * QK_ROPE
    if absorbed:
        outs = [heads_out(kv_rank), heads_out(QK_ROPE)]
    else:
        outs = [heads_out(cat), heads_out(cat)]
    outs += [rows_out(kv_rank, _F32), rows_out(QK_ROPE, _F32), rows_out(kv_rank, _BF16)]
    if absorbed:
        outs.append(rows_out(QK_ROPE, _BF16))
    outs.append(rows_out(pool_width, _F32))
    return pl.pallas_call(
        functools.partial(_inproj_kernel, q_rank=q_rank, kv_rank=kv_rank, absorbed=absorbed),
        grid=(t // tm,),
        in_specs=[
            pl.BlockSpec((tm, d), row),
            pl.BlockSpec(w_in.shape, const2),
            pl.BlockSpec((1, q_rank), const2),
            pl.BlockSpec((1, kv_rank), const2),
            pl.BlockSpec(w_q.shape, const2),
            pl.BlockSpec(w_uk.shape, lambda i: (0,) * w_uk.ndim),
            pl.BlockSpec((tm, cs.shape[1]), tab),
            pl.BlockSpec((tm, cc.shape[1]), tab),
            pl.BlockSpec((tm, ss.shape[1]), tab),
        ],
        out_specs=[o[0] for o in outs],
        out_shape=[o[1] for o in outs],
        compiler_params=_params(("parallel",)),
        name="inproj",
    )(x, w_in, g_q, g_kv, w_q, w_uk, cs, cc, ss)


def _attn_prompt_kernel(q_ref, k_ref, v_ref, o_ref, s0_ref, s1_ref, p0_ref, p1_ref, a0_ref, a1_ref,
                        m_ref, l_ref, acc_ref, *, tq, tk):
    i = pl.program_id(2)
    heads = q_ref.shape[0]
    m_ref[...] = jnp.full_like(m_ref, -jnp.inf)
    l_ref[...] = jnp.zeros_like(l_ref)
    acc_ref[...] = jnp.zeros_like(acc_ref)
    s_bufs = (s0_ref, s1_ref)
    p_bufs = ((p0_ref, a0_ref), (p1_ref, a1_ref))

    def keys(j):
        return pl.ds(pl.multiple_of(j * tk, tk), tk)

    def lanes(v, width):
        return jnp.concatenate([v] * (width // v.shape[1]), axis=1)

    def row_stat(v):
        return jnp.broadcast_to(v, (v.shape[0], STAT_LANES))

    def scores(j, b):
        for h in range(heads):
            s_bufs[b][h] = _dot_nt(q_ref[h], k_ref[h, keys(j), :])

    def softmax(j, b, masked=False):
        p_ref, a_ref = p_bufs[b]
        part = tq // SOFTMAX_ROW_PARTS
        for h in range(heads):
            for r in range(SOFTMAX_ROW_PARTS):
                rs = slice(r * part, (r + 1) * part)
                s = s_bufs[b][h, rs, :]
                if masked:
                    q_tok = lax.broadcasted_iota(jnp.int32, (part, 1), 0) + i * tq + r * part
                    k_tok = lax.broadcasted_iota(jnp.int32, (1, tk), 1) + j * tk
                    s = jnp.where(k_tok // CHUNK <= q_tok // CHUNK, s, -jnp.inf)
                m_old = m_ref[h, rs, :]
                m_new = jnp.maximum(m_old, row_stat(jnp.max(s, axis=1, keepdims=True)))
                p = jnp.exp2(s - lanes(m_new, tk))
                a = jnp.exp2(m_old - m_new)
                l_ref[h, rs, :] = a * l_ref[h, rs, :] + row_stat(jnp.sum(p, axis=1, keepdims=True))
                m_ref[h, rs, :] = m_new
                a_ref[h, rs, :] = a
                ps = slice(r * part // 2, (r + 1) * part // 2)
                p_ref[h, ps, :] = pltpu.bitcast(p.astype(_BF16), jnp.uint32)

    def values(j, b):
        p_ref, a_ref = p_bufs[b]
        for h in range(heads):
            acc_ref[h] = (lanes(a_ref[h], acc_ref.shape[2]) * acc_ref[h]
                          + _dot(pltpu.bitcast(p_ref[h], _BF16), v_ref[keys(j), :]))

    n_full = (i * tq) // tk
    scores(0, 0)

    @pl.when(n_full == 0)
    def _():
        softmax(0, 0, masked=True)
        values(0, 0)

    @pl.when(n_full > 0)
    def _():
        scores(1, 1)
        softmax(0, 0)

        def pair(k, carry):
            t = 2 * k + 2
            scores(t, 0)
            softmax(t - 1, 1)
            values(t - 2, 0)
            scores(t + 1, 1)
            softmax(t, 0)
            values(t - 1, 1)
            return carry

        lax.fori_loop(0, (n_full - 1) // 2, pair, 0)

        @pl.when(n_full % 2 == 1)
        def _():
            softmax(n_full, 1, masked=True)
            values(n_full - 1, 0)
            values(n_full, 1)

        @pl.when(n_full % 2 == 0)
        def _():
            scores(n_full, 0)
            softmax(n_full - 1, 1)
            values(n_full - 2, 0)
            softmax(n_full, 0, masked=True)
            values(n_full - 1, 1)
            values(n_full, 0)

    for h in range(heads):
        o_ref[h] = (acc_ref[h] / lanes(l_ref[h], acc_ref.shape[2])).astype(o_ref.dtype)


def _attn_prompt(qcat, kcat, ckvb, *, batch, seq, tq, tk, heads):
    assert tk % tq == 0 and tq % CHUNK == 0 and seq % tk == 0
    h, t, e = qcat.shape
    c = ckvb.shape[1]
    nq = seq // tq
    q_block = lambda b, hg, i: (hg, b * nq + i, 0)
    stat = pltpu.VMEM((heads, tq, STAT_LANES), _F32)
    tile = pltpu.VMEM((heads, tq, tk), _F32)
    packed = pltpu.VMEM((heads, tq // 2, tk), jnp.uint32)
    return pl.pallas_call(
        functools.partial(_attn_prompt_kernel, tq=tq, tk=tk),
        grid=(batch, h // heads, nq),
        in_specs=[
            pl.BlockSpec((heads, tq, e), q_block),
            pl.BlockSpec((heads, seq, e), lambda b, hg, i: (hg, b, 0)),
            pl.BlockSpec((seq, c), lambda b, hg, i: (b, 0)),
        ],
        out_specs=pl.BlockSpec((heads, tq, c), q_block),
        out_shape=jax.ShapeDtypeStruct((h, t, c), _BF16),
        scratch_shapes=[
            tile, tile,
            packed, packed,
            stat, stat,
            stat,
            stat,
            pltpu.VMEM((heads, tq, c), _F32),
        ],
        compiler_params=_params(("parallel", "parallel", "arbitrary")),
        name="attn_prompt",
    )(qcat, kcat, ckvb)


def _attn_sample_kernel(ql_ref, qr_ref, pkv_ref, pkr_ref, kv_ref, kr_ref, o_ref):
    rows = ql_ref.shape[0] * ql_ref.shape[1]
    ql = ql_ref[...].reshape(rows, ql_ref.shape[2])
    qr = qr_ref[...].reshape(rows, qr_ref.shape[2])
    pkv = pkv_ref[0].astype(_BF16)
    pkr = pkr_ref[0].astype(_BF16)
    kv = kv_ref[...]
    kr = kr_ref[...]
    s_past = _dot_nt(ql, pkv) + _dot_nt(qr, pkr)
    s_new = _dot_nt(ql, kv) + _dot_nt(qr, kr)
    m = jnp.maximum(jnp.max(s_past, axis=1, keepdims=True), jnp.max(s_new, axis=1, keepdims=True))
    p_past = jnp.exp2(s_past - m)
    p_new = jnp.exp2(s_new - m)
    l = jnp.sum(p_past, axis=1, keepdims=True) + jnp.sum(p_new, axis=1, keepdims=True)
    o = (_dot(p_past.astype(_BF16), pkv) + _dot(p_new.astype(_BF16), kv)) / l
    o_ref[...] = o.reshape(o_ref.shape).astype(o_ref.dtype)


def _attn_sample(qlat, qrope, past_kv, past_kr, ckvb, krb, *, seq):
    h, t, c = qlat.shape
    r = qrope.shape[2]
    batch, past_len, _ = past_kv.shape
    head_row = lambda b: (0, b, 0)
    return pl.pallas_call(
        _attn_sample_kernel,
        grid=(batch,),
        in_specs=[
            pl.BlockSpec((h, seq, c), head_row),
            pl.BlockSpec((h, seq, r), head_row),
            pl.BlockSpec((1, past_len, c), lambda b: (b, 0, 0)),
            pl.BlockSpec((1, past_len, r), lambda b: (b, 0, 0)),
            pl.BlockSpec((seq, c), lambda b: (b, 0)),
            pl.BlockSpec((seq, r), lambda b: (b, 0)),
        ],
        out_specs=pl.BlockSpec((h, seq, c), head_row),
        out_shape=jax.ShapeDtypeStruct((h, t, c), _BF16),
        compiler_params=_params(("parallel",)),
        name="attn_sample",
    )(qlat, qrope, past_kv, past_kr, ckvb, krb)


def _pool_delta(ext, pos0, group):
    n = ext.shape[0] - HALO
    pos = pos0 + lax.broadcasted_iota(jnp.int32, (n, 1), 0)
    out = []
    for gi, w in enumerate(POOL_WINDOWS):
        cols = ext[:, gi * group:(gi + 1) * group]
        s = cols
        shift = 1
        while shift < w:
            s = s + pltpu.roll(s, shift, axis=0)
            shift *= 2
        cnt = jnp.minimum(w, pos + 1).astype(_F32)
        out.append(s[HALO:] / cnt - cols[HALO:])
    return out


def _outproj_kernel(olat_ref, u_ref, halo_ref, x_ref, wuv_ref, wpool_ref, pscale_ref, wo_ref,
                    g_ref, b_ref, o_ref, *, alpha, seq, offset, blocks_per_seq):
    tm = u_ref.shape[0]
    group = u_ref.shape[1] // len(POOL_WINDOWS)
    u = u_ref[...]
    halo = halo_ref[...]
    if blocks_per_seq is not None:
        bi = pl.program_id(0) % blocks_per_seq
        halo = jnp.where(bi == 0, 0.0, halo)
        deltas = [_pool_delta(jnp.concatenate([halo, u], axis=0), offset + bi * tm, group)]
    else:
        deltas = []
        for s in range(tm // seq):
            ext = jnp.concatenate(
                [halo[s * HALO:(s + 1) * HALO], u[s * seq:(s + 1) * seq]], axis=0)
            deltas.append(_pool_delta(ext, offset, group))
    pooled = []
    for gi in range(len(POOL_WINDOWS)):
        d = jnp.concatenate([ds[gi] for ds in deltas], axis=0) if len(deltas) > 1 else deltas[0][gi]
        pooled.append(_dot(d.astype(_BF16), wpool_ref[gi]))
    o_pool = jnp.concatenate(pooled, axis=1) * pscale_ref[...]
    o_attn = jnp.concatenate(
        [_dot(olat_ref[h], wuv_ref[h]) for h in range(N_HEADS)], axis=1)
    mixed = jnp.concatenate([o_attn, o_pool], axis=1).astype(_BF16)
    for rs in _row_groups(tm):
        y = alpha * x_ref[rs, :] + _dot(mixed[rs], wo_ref[...])
        o_ref[rs, :] = _layer_norm(y, g_ref[...], b_ref[...])


def _outproj(olat, u, halo_src, x, w_uv, w_pool, pool_scale, w_o, g, b,
             *, alpha, seq, offset, tm, prompt):
    t, d = x.shape
    h, _, c = olat.shape
    pw = u.shape[1]
    row = lambda i: (i, 0)
    const2 = lambda i: (0, 0)
    const3 = lambda i: (0, 0, 0)
    if prompt:
        blocks_per_seq = seq // tm
        step = tm // HALO
        halo_spec = pl.BlockSpec((HALO, pw), lambda i: (jnp.maximum(i * step - 1, 0), 0))
    else:
        blocks_per_seq = None
        halo_spec = pl.BlockSpec(((tm // seq) * HALO, pw), row)
    return pl.pallas_call(
        functools.partial(_outproj_kernel, alpha=alpha, seq=seq, offset=offset,
                          blocks_per_seq=blocks_per_seq),
        grid=(t // tm,),
        in_specs=[
            pl.BlockSpec((h, tm, c), lambda i: (0, i, 0)),
            pl.BlockSpec((tm, pw), row),
            halo_spec,
            pl.BlockSpec((tm, d), row),
            pl.BlockSpec(w_uv.shape, const3),
            pl.BlockSpec(w_pool.shape, const3),
            pl.BlockSpec((1, pw), const2),
            pl.BlockSpec(w_o.shape, const2),
            pl.BlockSpec((1, d), const2),
            pl.BlockSpec((1, d), const2),
        ],
        out_specs=pl.BlockSpec((tm, d), row),
        out_shape=jax.ShapeDtypeStruct((t, d), _F32),
        compiler_params=_params(("parallel",)),
        name="outproj_prompt" if prompt else "outproj_sample",
    )(olat, u, halo_src, x, w_uv, w_pool, pool_scale, w_o, g, b)


def _ple_kernel(x_ref, p_ref, wg_ref, wp_ref, g_ref, b_ref, o_ref, *, alpha):
    for rs in _row_groups(x_ref.shape[0]):
        x = x_ref[rs, :]
        gate = jax.nn.sigmoid(_dot(x.astype(_BF16), wg_ref[...]))
        proj = _dot(p_ref[rs, :].astype(_BF16), wp_ref[...])
        o_ref[rs, :] = _layer_norm(alpha * x + gate * proj, g_ref[...], b_ref[...])


def _ple(x, p, w_gate, w_proj, g, b, *, alpha, tm):
    t, d = x.shape
    pd = p.shape[1]
    row = lambda i: (i, 0)
    const2 = lambda i: (0, 0)
    return pl.pallas_call(
        functools.partial(_ple_kernel, alpha=alpha),
        grid=(t // tm,),
        in_specs=[
            pl.BlockSpec((tm, d), row),
            pl.BlockSpec((tm, pd), row),
            pl.BlockSpec(w_gate.shape, const2),
            pl.BlockSpec(w_proj.shape, const2),
            pl.BlockSpec((1, d), const2),
            pl.BlockSpec((1, d), const2),
        ],
        out_specs=pl.BlockSpec((tm, d), row),
        out_shape=jax.ShapeDtypeStruct((t, d), _F32),
        compiler_params=_params(("parallel",)),
        name="ple",
    )(x, p, w_gate, w_proj, g, b)


def _rotate_half_cols(w):
    half = w.shape[-1] // 2
    return jnp.concatenate([-w[..., half:], w[..., :half]], axis=-1)


def _rope_tables(pos, reps):
    inv = 1.0 / (ROPE_THETA ** (jnp.arange(0, QK_ROPE, 2, dtype=_F32) / QK_ROPE))
    ang = pos.astype(_F32)[:, None] * inv[None, :]
    ang = jnp.concatenate([ang, ang], -1)
    cos, sin = jnp.cos(ang), jnp.sin(ang)
    tile = lambda a: jnp.tile(a, (reps, 1))
    return (tile(jnp.concatenate([cos, sin], -1)), tile(jnp.concatenate([cos, cos], -1)),
            tile(jnp.concatenate([sin, sin], -1)))


def _tile_rows(total, preferred):
    tile = min(total, preferred)
    assert total % tile == 0
    return tile


def _prep_layer_weights(w):
    q_rank = w['g_q'].shape[0]
    kv_rank = w['g_kv'].shape[0]
    w_in = w['w_in']
    o_kr = q_rank + kv_rank
    w_kr = w_in[:, o_kr:o_kr + QK_ROPE]
    w_in_all = jnp.concatenate(
        [w_in[:, :o_kr], w_kr, _rotate_half_cols(w_kr), w_in[:, o_kr + QK_ROPE:]], axis=1)
    w_q = jnp.concatenate(
        [w['w_uq_nope'].reshape(q_rank, -1), w['w_uq_rope'].reshape(q_rank, -1),
         _rotate_half_cols(w['w_uq_rope']).reshape(q_rank, -1)], axis=1)
    row = lambda v: v.reshape(1, -1)
    return dict(
        ffn1_gu=w['ffn1_gu'].astype(_BF16), ffn1_down=w['ffn1_down'].astype(_BF16),
        ffn2_gu=w['ffn2_gu'].astype(_BF16), ffn2_down=w['ffn2_down'].astype(_BF16),
        w_in=w_in_all.astype(_BF16), w_q=w_q.astype(_BF16),
        w_ukt=jnp.transpose(w['w_uk'], (1, 2, 0)).astype(_BF16),
        w_uk=w['w_uk'].reshape(kv_rank, -1).astype(_BF16),
        w_uv=jnp.transpose(w['w_uv'], (1, 0, 2)).astype(_BF16),
        w_pool=w['w_pool'].astype(_BF16), pool_scale=row(w['pool_scale']),
        w_o=w['w_o'].astype(_BF16),
        w_ple_gate=w['w_ple_gate'].astype(_BF16), w_ple_proj=w['w_ple_proj'].astype(_BF16),
        g_q=row(w['g_q']), g_kv=row(w['g_kv']),
        ln1=(row(w['ln1_g']), row(w['ln1_b'])), ln2=(row(w['ln2_g']), row(w['ln2_b'])),
        ln3=(row(w['ln3_g']), row(w['ln3_b'])), ln4=(row(w['ln4_g']), row(w['ln4_b'])),
    )


def _encoder_layer(x, p, past, pool_hist, offset, pw, *, alpha):
    bsz, seq, d = x.shape
    t = bsz * seq
    prompt = past is None
    xf = x.reshape(t, d)
    tm = _tile_rows(t, 512)
    d_ff = pw['ffn1_down'].shape[0]
    tf = 512 if d_ff % 512 == 0 else d_ff

    x1 = _ffn_ln(xf, pw['ffn1_gu'], pw['ffn1_down'], *pw['ln1'], alpha=alpha, tm=tm, tf=tf)

    tmi = _tile_rows(t, 256)
    if prompt:
        assert seq % tmi == 0
        tabs = _rope_tables(offset + jnp.arange(seq), 1)
    else:
        assert tmi % seq == 0
        tabs = _rope_tables(offset + jnp.arange(seq), tmi // seq)
    w_uk = pw['w_uk'] if prompt else pw['w_ukt']
    proj = _inproj(x1, pw['w_in'], pw['g_q'], pw['g_kv'], pw['w_q'], w_uk, *tabs, tm=tmi,
                   absorbed=not prompt)
    if prompt:
        qcat, kcat, ckv, kr, ckvb, u = proj
        tq = tk = min(seq, 512)
        olat = _attn_prompt(qcat, kcat, ckvb, batch=bsz, seq=seq, tq=tq, tk=tk, heads=2)
        halo_src = u
    else:
        qlat, qrope, ckv, kr, ckvb, krb, u = proj
        olat = _attn_sample(qlat, qrope, past[0], past[1], ckvb, krb, seq=seq)
        halo_src = jnp.pad(pool_hist, ((0, 0), (HALO - POOL_HIST, 0), (0, 0))).reshape(bsz * HALO, -1)

    tmo = _tile_rows(t, 512 if prompt else 256)
    if prompt:
        assert seq % tmo == 0 and tmo % HALO == 0
    else:
        assert tmo % seq == 0
    x2 = _outproj(olat, u, halo_src, x1, pw['w_uv'], pw['w_pool'], pw['pool_scale'], pw['w_o'],
                  *pw['ln2'], alpha=alpha, seq=seq, offset=offset, tm=tmo, prompt=prompt)
    x3 = _ffn_ln(x2, pw['ffn2_gu'], pw['ffn2_down'], *pw['ln3'], alpha=alpha, tm=tm, tf=tf)
    y = _ple(x3, p.reshape(t, -1), pw['w_ple_gate'], pw['w_ple_proj'], *pw['ln4'],
             alpha=alpha, tm=tm)

    u3 = u.reshape(bsz, seq, -1)
    if seq >= POOL_HIST:
        new_hist = u3[:, seq - POOL_HIST:]
    else:
        new_hist = jnp.concatenate([pool_hist, u3], axis=1)[:, -POOL_HIST:]
    return (y.reshape(bsz, seq, d), ckv.reshape(bsz, seq, -1), kr.reshape(bsz, seq, -1), new_hist)


def kernel(x_prompt, x_sample, cache_kv_latent, cache_k_rope, state_pool, p_prompt, p_sample, ffn1_gu, ffn1_down, ln1_g, ln1_b, w_in, g_q, g_kv, w_uq_nope, w_uq_rope, w_uk, w_uv, w_pool, pool_scale, w_o, ln2_g, ln2_b, ffn2_gu, ffn2_down, ln3_g, ln3_b, w_ple_gate, w_ple_proj, ln4_g, ln4_b):
    depth = ffn1_gu.shape[0]
    past_len = cache_kv_latent.shape[2]
    alpha = (2.0 * depth) ** 0.25
    stacked = dict(
        ffn1_gu=ffn1_gu, ffn1_down=ffn1_down, ln1_g=ln1_g, ln1_b=ln1_b, w_in=w_in, g_q=g_q,
        g_kv=g_kv, w_uq_nope=w_uq_nope, w_uq_rope=w_uq_rope, w_uk=w_uk, w_uv=w_uv,
        w_pool=w_pool, pool_scale=pool_scale, w_o=w_o, ln2_g=ln2_g, ln2_b=ln2_b,
        ffn2_gu=ffn2_gu, ffn2_down=ffn2_down, ln3_g=ln3_g, ln3_b=ln3_b,
        w_ple_gate=w_ple_gate, w_ple_proj=w_ple_proj, ln4_g=ln4_g, ln4_b=ln4_b)
    hp, hs = x_prompt, x_sample
    outs = [[] for _ in range(6)]
    for i in range(depth):
        pw = _prep_layer_weights({k: v[i] for k, v in stacked.items()})
        hp, c1, k1, s1 = _encoder_layer(hp, p_prompt[i], None, None, 0, pw, alpha=alpha)
        hs, c2, k2, s2 = _encoder_layer(
            hs, p_sample[i], (cache_kv_latent[i], cache_k_rope[i]), state_pool[i], past_len, pw,
            alpha=alpha)
        for acc, v in zip(outs, (c1, k1, s1, c2, k2, s2)):
            acc.append(v)
    return (hp, hs) + tuple(jnp.stack(v) for v in outs)
```

```python
import functools
import math

import jax
import jax.numpy as jnp
from jax import lax
from jax.experimental import pallas as pl
from jax.experimental.pallas import tpu as pltpu

CHUNK = 64
N_HEADS = 8
QK_NOPE = 128
QK_ROPE = 64
V_HEAD = 128
POOL_WINDOWS = (2, 4, 8, 16)
POOL_HIST = max(POOL_WINDOWS) - 1
HALO = POOL_HIST + 1
ROPE_THETA = 10000.0
LN_EPS = 1e-5
RMS_EPS = 1e-6
ATTN_SCALE = (QK_NOPE + QK_ROPE) ** -0.5
SCORE_SCALE = ATTN_SCALE * math.log2(math.e)

V7X_VMEM_LIMIT_BYTES = 56 * 1024 * 1024
STAT_LANES = 128
SOFTMAX_ROW_PARTS = 4

_F32 = jnp.float32
_BF16 = jnp.bfloat16
_NT_DIMS = (((1,), (1,)), ((), ()))


def _dot(a, b):
    return jnp.dot(a, b, preferred_element_type=_F32)


def _dot_nt(a, b):
    return lax.dot_general(a, b, _NT_DIMS, preferred_element_type=_F32)


def _layer_norm(y, g, b):
    mu = jnp.mean(y, axis=-1, keepdims=True)
    d = y - mu
    var = jnp.mean(d * d, axis=-1, keepdims=True)
    return d * lax.rsqrt(var + LN_EPS) * g + b


def _rms_norm(z, g):
    return z * lax.rsqrt(jnp.mean(z * z, axis=-1, keepdims=True) + RMS_EPS) * g


def _row_groups(rows, group_rows=256):
    group_rows = min(rows, group_rows)
    return [slice(r, r + group_rows) for r in range(0, rows, group_rows)]


def _params(semantics):
    return pltpu.CompilerParams(
        dimension_semantics=semantics, vmem_limit_bytes=V7X_VMEM_LIMIT_BYTES)


def _ffn_ln_kernel(x_ref, wg_ref, wu_ref, wd_ref, g_ref, b_ref, o_ref, xb_ref, *, alpha):
    j = pl.program_id(1)

    @pl.when(j == 0)
    def _():
        xb_ref[...] = x_ref[...].astype(_BF16)
        o_ref[...] = jnp.zeros_like(o_ref)

    xb = xb_ref[...]
    gate = _dot(xb, wg_ref[...])
    up = _dot(xb, wu_ref[...])
    act = (gate * jax.nn.sigmoid(gate) * up).astype(_BF16)
    o_ref[...] += _dot(act, wd_ref[...])

    @pl.when(j == pl.num_programs(1) - 1)
    def _():
        y = alpha * x_ref[...] + 0.5 * o_ref[...]
        o_ref[...] = _layer_norm(y, g_ref[...], b_ref[...])


def _ffn_ln(x, w_gu, w_down, g, b, *, alpha, tm, tf):
    t, d = x.shape
    f = w_down.shape[0]
    nj = f // tf
    return pl.pallas_call(
        functools.partial(_ffn_ln_kernel, alpha=alpha),
        grid=(t // tm, nj),
        in_specs=[
            pl.BlockSpec((tm, d), lambda i, j: (i, 0)),
            pl.BlockSpec((d, tf), lambda i, j: (0, j)),
            pl.BlockSpec((d, tf), lambda i, j: (0, j + nj)),
            pl.BlockSpec((tf, d), lambda i, j: (j, 0)),
            pl.BlockSpec((1, d), lambda i, j: (0, 0)),
            pl.BlockSpec((1, d), lambda i, j: (0, 0)),
        ],
        out_specs=pl.BlockSpec((tm, d), lambda i, j: (i, 0)),
        out_shape=jax.ShapeDtypeStruct((t, d), _F32),
        scratch_shapes=[pltpu.VMEM((tm, d), _BF16)],
        compiler_params=_params(("parallel", "arbitrary")),
        name="ffn_ln",
    )(x, w_gu, w_gu, w_down, g, b)


def _inproj_kernel(x_ref, win_ref, gq_ref, gkv_ref, wq_ref, wuk_ref, cs_ref,
                   *out_refs, q_rank, kv_rank, absorbed):
    if absorbed:
        qlat_ref, qrope_ref, ckv_ref, kr_ref, ckvb_ref, krb_ref, u_ref = out_refs
    else:
        qcat_ref, kcat_ref, ckv_ref, kr_ref, ckvb_ref, u_ref = out_refs
    xb = x_ref[...].astype(_BF16)
    z = _dot(xb, win_ref[...])
    o_kv = q_rank
    o_kr = o_kv + kv_rank
    o_u = o_kr + 2 * QK_ROPE
    u_ref[...] = z[:, o_u:]

    c_kv = _rms_norm(z[:, o_kv:o_kr], gkv_ref[...])
    c_kv_b = c_kv.astype(_BF16)
    ckv_ref[...] = c_kv
    ckvb_ref[...] = c_kv_b

    kt = z[:, o_kr:o_u] * cs_ref[...]
    k2 = kt + pltpu.roll(kt, QK_ROPE, axis=1)
    kr_ref[...] = k2[:, :QK_ROPE]

    c_q = _rms_norm(z[:, :q_rank], gq_ref[...]).astype(_BF16)
    qz = _dot(c_q, wq_ref[...])
    n_nope = N_HEADS * QK_NOPE
    n_rope = N_HEADS * QK_ROPE
    cs = cs_ref[...]
    sc = pltpu.roll(cs, QK_ROPE, axis=1)
    first = lax.broadcasted_iota(jnp.int32, cs.shape, 1) < QK_ROPE
    lane_reps = n_rope // cs.shape[1]
    cos = jnp.concatenate([jnp.where(first, cs, sc)] * lane_reps, axis=1)
    sin = jnp.concatenate([jnp.where(first, sc, cs)] * lane_reps, axis=1)
    q_rope = (qz[:, n_nope:n_nope + n_rope] * cos + qz[:, n_nope + n_rope:] * sin) * SCORE_SCALE
    if absorbed:
        krb_ref[...] = k2[:, :QK_ROPE].astype(_BF16)
        q_rope = q_rope.astype(_BF16)
        q_nope = qz[:, :n_nope].astype(_BF16)
        for h in range(N_HEADS):
            hs = slice(h * QK_NOPE, (h + 1) * QK_NOPE)
            q_lat = _dot_nt(q_nope[:, hs], wuk_ref[:, hs]) * SCORE_SCALE
            qlat_ref[h] = q_lat.astype(_BF16)
            qrope_ref[h] = q_rope[:, h * QK_ROPE:(h + 1) * QK_ROPE]
    else:
        pair = 2 * QK_ROPE
        low = lax.broadcasted_iota(jnp.int32, (x_ref.shape[0], pair), 1) < QK_ROPE
        k_tail = jnp.where(low, k2, 0.0).astype(_BF16)
        q_nope = (qz[:, :n_nope] * SCORE_SCALE).astype(_BF16)
        k_nope = _dot(c_kv_b, wuk_ref[...]).astype(_BF16)
        for h in range(N_HEADS):
            v = q_rope[:, (h // 2) * pair:(h // 2 + 1) * pair]
            if h % 2:
                v = pltpu.roll(v, QK_ROPE, axis=1)
            q_tail = jnp.where(low, v, 0.0).astype(_BF16)
            hs = slice(h * QK_NOPE, (h + 1) * QK_NOPE)
            qcat_ref[h] = jnp.concatenate([q_nope[:, hs], q_tail], axis=1)
            kcat_ref[h] = jnp.concatenate([k_nope[:, hs], k_tail], axis=1)


def _inproj(x, w_in, g_q, g_kv, w_q, w_uk, cs, *, tm, absorbed):
    t, d = x.shape
    q_rank = g_q.shape[1]
    kv_rank = g_kv.shape[1]
    pool_width = w_in.shape[1] - q_rank - kv_rank - 2 * QK_ROPE
    nper = cs.shape[0] // tm
    const2 = lambda i: (0, 0)
    row = lambda i: (i, 0)
    tab = lambda i: (i % nper, 0)
    head_row = lambda i: (0, i, 0)

    def rows_out(width, dtype):
        return pl.BlockSpec((tm, width), row), jax.ShapeDtypeStruct((t, width), dtype)

    def heads_out(width):
        return (pl.BlockSpec((N_HEADS, tm, width), head_row),
                jax.ShapeDtypeStruct((N_HEADS, t, width), _BF16))

    cat = QK_NOPE + 2 * QK_ROPE
    if absorbed:
        outs = [heads_out(kv_rank), heads_out(QK_ROPE)]
    else:
        outs = [heads_out(cat), heads_out(cat)]
    outs += [rows_out(kv_rank, _F32), rows_out(QK_ROPE, _F32), rows_out(kv_rank, _BF16)]
    if absorbed:
        outs.append(rows_out(QK_ROPE, _BF16))
    outs.append(rows_out(pool_width, _F32))
    return pl.pallas_call(
        functools.partial(_inproj_kernel, q_rank=q_rank, kv_rank=kv_rank, absorbed=absorbed),
        grid=(t // tm,),
        in_specs=[
            pl.BlockSpec((tm, d), row),
            pl.BlockSpec(w_in.shape, const2),
            pl.BlockSpec((1, q_rank), const2),
            pl.BlockSpec((1, kv_rank), const2),
            pl.BlockSpec(w_q.shape, const2),
            pl.BlockSpec(w_uk.shape, const2),
            pl.BlockSpec((tm, cs.shape[1]), tab),
        ],
        out_specs=[o[0] for o in outs],
        out_shape=[o[1] for o in outs],
        compiler_params=_params(("parallel",)),
        name="inproj",
    )(x, w_in, g_q, g_kv, w_q, w_uk, cs)


def _attn_prompt_kernel(q_ref, k_ref, v_ref, o_ref, s0_ref, s1_ref, p0_ref, p1_ref, a0_ref, a1_ref,
                        m_ref, l_ref, acc_ref, *, tq, tk):
    i = pl.program_id(2)
    heads = q_ref.shape[0]
    m_ref[...] = jnp.full_like(m_ref, -jnp.inf)
    l_ref[...] = jnp.zeros_like(l_ref)
    acc_ref[...] = jnp.zeros_like(acc_ref)
    s_bufs = (s0_ref, s1_ref)
    p_bufs = ((p0_ref, a0_ref), (p1_ref, a1_ref))

    def keys(j):
        return pl.ds(pl.multiple_of(j * tk, tk), tk)

    def lanes(v, width):
        return jnp.concatenate([v] * (width // v.shape[1]), axis=1)

    def row_stat(v):
        return jnp.broadcast_to(v, (v.shape[0], STAT_LANES))

    def scores(j, b):
        for h in range(heads):
            s_bufs[b][h] = _dot_nt(q_ref[h], k_ref[h, keys(j), :])

    def softmax(j, b, masked=False):
        p_ref, a_ref = p_bufs[b]
        part = tq // SOFTMAX_ROW_PARTS
        for h in range(heads):
            for r in range(SOFTMAX_ROW_PARTS):
                rs = slice(r * part, (r + 1) * part)
                s = s_bufs[b][h, rs, :]
                if masked:
                    q_tok = lax.broadcasted_iota(jnp.int32, (part, 1), 0) + i * tq + r * part
                    k_tok = lax.broadcasted_iota(jnp.int32, (1, tk), 1) + j * tk
                    s = jnp.where(k_tok // CHUNK <= q_tok // CHUNK, s, -jnp.inf)
                m_old = m_ref[h, rs, :]
                m_new = jnp.maximum(m_old, row_stat(jnp.max(s, axis=1, keepdims=True)))
                p = jnp.exp2(s - lanes(m_new, tk))
                a = jnp.exp2(m_old - m_new)
                l_ref[h, rs, :] = a * l_ref[h, rs, :] + row_stat(jnp.sum(p, axis=1, keepdims=True))
                m_ref[h, rs, :] = m_new
                a_ref[h, rs, :] = a
                ps = slice(r * part // 2, (r + 1) * part // 2)
                p_ref[h, ps, :] = pltpu.bitcast(p.astype(_BF16), jnp.uint32)

    def values(j, b):
        p_ref, a_ref = p_bufs[b]
        for h in range(heads):
            acc_ref[h] = (lanes(a_ref[h], acc_ref.shape[2]) * acc_ref[h]
                          + _dot(pltpu.bitcast(p_ref[h], _BF16), v_ref[keys(j), :]))

    n_full = (i * tq) // tk
    scores(0, 0)

    @pl.when(n_full == 0)
    def _():
        softmax(0, 0, masked=True)
        values(0, 0)

    @pl.when(n_full > 0)
    def _():
        scores(1, 1)
        softmax(0, 0)

        def pair(k, carry):
            t = 2 * k + 2
            scores(t, 0)
            softmax(t - 1, 1)
            values(t - 2, 0)
            scores(t + 1, 1)
            softmax(t, 0)
            values(t - 1, 1)
            return carry

        lax.fori_loop(0, (n_full - 1) // 2, pair, 0)

        @pl.when(n_full % 2 == 1)
        def _():
            softmax(n_full, 1, masked=True)
            values(n_full - 1, 0)
            values(n_full, 1)

        @pl.when(n_full % 2 == 0)
        def _():
            scores(n_full, 0)
            softmax(n_full - 1, 1)
            values(n_full - 2, 0)
            softmax(n_full, 0, masked=True)
            values(n_full - 1, 1)
            values(n_full, 0)

    for h in range(heads):
        o_ref[h] = (acc_ref[h] / lanes(l_ref[h], acc_ref.shape[2])).astype(o_ref.dtype)


def _attn_prompt(qcat, kcat, ckvb, *, batch, seq, tq, tk, heads):
    assert tk % tq == 0 and tq % CHUNK == 0 and seq % tk == 0
    h, t, e = qcat.shape
    c = ckvb.shape[1]
    nq = seq // tq
    q_block = lambda b, hg, i: (hg, b * nq + i, 0)
    stat = pltpu.VMEM((heads, tq, STAT_LANES), _F32)
    tile = pltpu.VMEM((heads, tq, tk), _F32)
    packed = pltpu.VMEM((heads, tq // 2, tk), jnp.uint32)
    return pl.pallas_call(
        functools.partial(_attn_prompt_kernel, tq=tq, tk=tk),
        grid=(batch, h // heads, nq),
        in_specs=[
            pl.BlockSpec((heads, tq, e), q_block),
            pl.BlockSpec((heads, seq, e), lambda b, hg, i: (hg, b, 0)),
            pl.BlockSpec((seq, c), lambda b, hg, i: (b, 0)),
        ],
        out_specs=pl.BlockSpec((heads, tq, c), q_block),
        out_shape=jax.ShapeDtypeStruct((h, t, c), _BF16),
        scratch_shapes=[
            tile, tile,
            packed, packed,
            stat, stat,
            stat,
            stat,
            pltpu.VMEM((heads, tq, c), _F32),
        ],
        compiler_params=_params(("parallel", "parallel", "arbitrary")),
        name="attn_prompt",
    )(qcat, kcat, ckvb)


def _attn_sample_kernel(ql_ref, qr_ref, pkv_ref, pkr_ref, kv_ref, kr_ref, o_ref):
    rows = ql_ref.shape[0] * ql_ref.shape[1]
    ql = ql_ref[...].reshape(rows, ql_ref.shape[2])
    qr = qr_ref[...].reshape(rows, qr_ref.shape[2])
    pkv = pkv_ref[0].astype(_BF16)
    pkr = pkr_ref[0].astype(_BF16)
    kv = kv_ref[...]
    kr = kr_ref[...]
    s_past = _dot_nt(ql, pkv) + _dot_nt(qr, pkr)
    s_new = _dot_nt(ql, kv) + _dot_nt(qr, kr)
    m = jnp.maximum(jnp.max(s_past, axis=1, keepdims=True), jnp.max(s_new, axis=1, keepdims=True))
    p_past = jnp.exp2(s_past - m)
    p_new = jnp.exp2(s_new - m)
    l = jnp.sum(p_past, axis=1, keepdims=True) + jnp.sum(p_new, axis=1, keepdims=True)
    o = (_dot(p_past.astype(_BF16), pkv) + _dot(p_new.astype(_BF16), kv)) / l
    o_ref[...] = o.reshape(o_ref.shape).astype(o_ref.dtype)


def _attn_sample(qlat, qrope, past_kv, past_kr, ckvb, krb, *, seq):
    h, t, c = qlat.shape
    r = qrope.shape[2]
    batch, past_len, _ = past_kv.shape
    head_row = lambda b: (0, b, 0)
    return pl.pallas_call(
        _attn_sample_kernel,
        grid=(batch,),
        in_specs=[
            pl.BlockSpec((h, seq, c), head_row),
            pl.BlockSpec((h, seq, r), head_row),
            pl.BlockSpec((1, past_len, c), lambda b: (b, 0, 0)),
            pl.BlockSpec((1, past_len, r), lambda b: (b, 0, 0)),
            pl.BlockSpec((seq, c), lambda b: (b, 0)),
            pl.BlockSpec((seq, r), lambda b: (b, 0)),
        ],
        out_specs=pl.BlockSpec((h, seq, c), head_row),
        out_shape=jax.ShapeDtypeStruct((h, t, c), _BF16),
        compiler_params=_params(("parallel",)),
        name="attn_sample",
    )(qlat, qrope, past_kv, past_kr, ckvb, krb)


def _pool_delta(ext, pos0, group):
    n = ext.shape[0] - HALO
    pos = pos0 + lax.broadcasted_iota(jnp.int32, (n, 1), 0)
    out = []
    for gi, w in enumerate(POOL_WINDOWS):
        cols = ext[:, gi * group:(gi + 1) * group]
        s = cols
        shift = 1
        while shift < w:
            s = s + pltpu.roll(s, shift, axis=0)
            shift *= 2
        cnt = jnp.minimum(w, pos + 1).astype(_F32)
        out.append(s[HALO:] / cnt - cols[HALO:])
    return out


def _outproj_kernel(olat_ref, u_ref, halo_ref, x_ref, wuv_ref, wpool_ref, pscale_ref, wo_ref,
                    g_ref, b_ref, o_ref, *, alpha, seq, offset, blocks_per_seq):
    tm = u_ref.shape[0]
    group = u_ref.shape[1] // len(POOL_WINDOWS)
    u = u_ref[...]
    halo = halo_ref[...]
    if blocks_per_seq is not None:
        bi = pl.program_id(0) % blocks_per_seq
        halo = jnp.where(bi == 0, 0.0, halo)
        deltas = [_pool_delta(jnp.concatenate([halo, u], axis=0), offset + bi * tm, group)]
    else:
        deltas = []
        for s in range(tm // seq):
            ext = jnp.concatenate(
                [halo[s * HALO:(s + 1) * HALO], u[s * seq:(s + 1) * seq]], axis=0)
            deltas.append(_pool_delta(ext, offset, group))
    pooled = []
    for gi in range(len(POOL_WINDOWS)):
        d = jnp.concatenate([ds[gi] for ds in deltas], axis=0) if len(deltas) > 1 else deltas[0][gi]
        pooled.append(_dot(d.astype(_BF16), wpool_ref[gi]))
    o_pool = jnp.concatenate(pooled, axis=1) * pscale_ref[...]
    o_attn = jnp.concatenate(
        [_dot(olat_ref[h], wuv_ref[:, h * V_HEAD:(h + 1) * V_HEAD]) for h in range(N_HEADS)], axis=1)
    mixed = jnp.concatenate([o_attn, o_pool], axis=1).astype(_BF16)
    for rs in _row_groups(tm):
        y = alpha * x_ref[rs, :] + _dot(mixed[rs], wo_ref[...])
        o_ref[rs, :] = _layer_norm(y, g_ref[...], b_ref[...])


def _outproj(olat, u, halo_src, x, w_uv, w_pool, pool_scale, w_o, g, b,
             *, alpha, seq, offset, tm, prompt):
    t, d = x.shape
    h, _, c = olat.shape
    pw = u.shape[1]
    row = lambda i: (i, 0)
    const2 = lambda i: (0, 0)
    const3 = lambda i: (0, 0, 0)
    if prompt:
        blocks_per_seq = seq // tm
        step = tm // HALO
        halo_spec = pl.BlockSpec((HALO, pw), lambda i: (jnp.maximum(i * step - 1, 0), 0))
    else:
        blocks_per_seq = None
        halo_spec = pl.BlockSpec(((tm // seq) * HALO, pw), row)
    return pl.pallas_call(
        functools.partial(_outproj_kernel, alpha=alpha, seq=seq, offset=offset,
                          blocks_per_seq=blocks_per_seq),
        grid=(t // tm,),
        in_specs=[
            pl.BlockSpec((h, tm, c), lambda i: (0, i, 0)),
            pl.BlockSpec((tm, pw), row),
            halo_spec,
            pl.BlockSpec((tm, d), row),
            pl.BlockSpec(w_uv.shape, const2),
            pl.BlockSpec(w_pool.shape, const3),
            pl.BlockSpec((1, pw), const2),
            pl.BlockSpec(w_o.shape, const2),
            pl.BlockSpec((1, d), const2),
            pl.BlockSpec((1, d), const2),
        ],
        out_specs=pl.BlockSpec((tm, d), row),
        out_shape=jax.ShapeDtypeStruct((t, d), _F32),
        compiler_params=_params(("parallel",)),
        name="outproj_prompt" if prompt else "outproj_sample",
    )(olat, u, halo_src, x, w_uv, w_pool, pool_scale, w_o, g, b)


def _ple_kernel(x_ref, p_ref, wg_ref, wp_ref, g_ref, b_ref, o_ref, *, alpha):
    for rs in _row_groups(x_ref.shape[0]):
        x = x_ref[rs, :]
        gate = jax.nn.sigmoid(_dot(x.astype(_BF16), wg_ref[...]))
        proj = _dot(p_ref[rs, :].astype(_BF16), wp_ref[...])
        o_ref[rs, :] = _layer_norm(alpha * x + gate * proj, g_ref[...], b_ref[...])


def _ple(x, p, w_gate, w_proj, g, b, *, alpha, tm):
    t, d = x.shape
    pd = p.shape[1]
    row = lambda i: (i, 0)
    const2 = lambda i: (0, 0)
    return pl.pallas_call(
        functools.partial(_ple_kernel, alpha=alpha),
        grid=(t // tm,),
        in_specs=[
            pl.BlockSpec((tm, d), row),
            pl.BlockSpec((tm, pd), row),
            pl.BlockSpec(w_gate.shape, const2),
            pl.BlockSpec(w_proj.shape, const2),
            pl.BlockSpec((1, d), const2),
            pl.BlockSpec((1, d), const2),
        ],
        out_specs=pl.BlockSpec((tm, d), row),
        out_shape=jax.ShapeDtypeStruct((t, d), _F32),
        compiler_params=_params(("parallel",)),
        name="ple",
    )(x, p, w_gate, w_proj, g, b)


def _rotate_half_cols(w):
    half = w.shape[-1] // 2
    return jnp.concatenate([-w[..., half:], w[..., :half]], axis=-1)


def _rope_table(pos, reps):
    inv = 1.0 / (ROPE_THETA ** (jnp.arange(0, QK_ROPE, 2, dtype=_F32) / QK_ROPE))
    ang = pos.astype(_F32)[:, None] * inv[None, :]
    ang = jnp.concatenate([ang, ang], -1)
    return jnp.tile(jnp.concatenate([jnp.cos(ang), jnp.sin(ang)], -1), (reps, 1))


def _tile_rows(total, preferred):
    tile = min(total, preferred)
    assert total % tile == 0
    return tile


def _prep_layer_weights(w):
    q_rank = w['g_q'].shape[0]
    kv_rank = w['g_kv'].shape[0]
    w_in = w['w_in']
    o_kr = q_rank + kv_rank
    w_kr = w_in[:, o_kr:o_kr + QK_ROPE]
    w_in_all = jnp.concatenate(
        [w_in[:, :o_kr], w_kr, _rotate_half_cols(w_kr), w_in[:, o_kr + QK_ROPE:]], axis=1)
    w_q = jnp.concatenate(
        [w['w_uq_nope'].reshape(q_rank, -1), w['w_uq_rope'].reshape(q_rank, -1),
         _rotate_half_cols(w['w_uq_rope']).reshape(q_rank, -1)], axis=1)
    row = lambda v: v.reshape(1, -1)
    return dict(
        ffn1_gu=w['ffn1_gu'].astype(_BF16), ffn1_down=w['ffn1_down'].astype(_BF16),
        ffn2_gu=w['ffn2_gu'].astype(_BF16), ffn2_down=w['ffn2_down'].astype(_BF16),
        w_in=w_in_all.astype(_BF16), w_q=w_q.astype(_BF16),
        w_uk=w['w_uk'].reshape(kv_rank, -1).astype(_BF16),
        w_uv=w['w_uv'].reshape(kv_rank, -1).astype(_BF16),
        w_pool=w['w_pool'].astype(_BF16), pool_scale=row(w['pool_scale']),
        w_o=w['w_o'].astype(_BF16),
        w_ple_gate=w['w_ple_gate'].astype(_BF16), w_ple_proj=w['w_ple_proj'].astype(_BF16),
        g_q=row(w['g_q']), g_kv=row(w['g_kv']),
        ln1=(row(w['ln1_g']), row(w['ln1_b'])), ln2=(row(w['ln2_g']), row(w['ln2_b'])),
        ln3=(row(w['ln3_g']), row(w['ln3_b'])), ln4=(row(w['ln4_g']), row(w['ln4_b'])),
    )


def _encoder_layer(x, p, past, pool_hist, offset, pw, *, alpha):
    bsz, seq, d = x.shape
    t = bsz * seq
    prompt = past is None
    xf = x.reshape(t, d)
    tm = _tile_rows(t, 512)
    d_ff = pw['ffn1_down'].shape[0]
    tf = 512 if d_ff % 512 == 0 else d_ff

    x1 = _ffn_ln(xf, pw['ffn1_gu'], pw['ffn1_down'], *pw['ln1'], alpha=alpha, tm=tm, tf=tf)

    tmi = _tile_rows(t, 256)
    if prompt:
        assert seq % tmi == 0
        tab = _rope_table(offset + jnp.arange(seq), 1)
    else:
        assert tmi % seq == 0
        tab = _rope_table(offset + jnp.arange(seq), tmi // seq)
    proj = _inproj(x1, pw['w_in'], pw['g_q'], pw['g_kv'], pw['w_q'], pw['w_uk'], tab, tm=tmi,
                   absorbed=not prompt)
    if prompt:
        qcat, kcat, ckv, kr, ckvb, u = proj
        tq = tk = min(seq, 512)
        olat = _attn_prompt(qcat, kcat, ckvb, batch=bsz, seq=seq, tq=tq, tk=tk, heads=2)
        halo_src = u
    else:
        qlat, qrope, ckv, kr, ckvb, krb, u = proj
        olat = _attn_sample(qlat, qrope, past[0], past[1], ckvb, krb, seq=seq)
        halo_src = jnp.pad(pool_hist, ((0, 0), (HALO - POOL_HIST, 0), (0, 0))).reshape(bsz * HALO, -1)

    tmo = _tile_rows(t, 512 if prompt else 256)
    if prompt:
        assert seq % tmo == 0 and tmo % HALO == 0
    else:
        assert tmo % seq == 0
    x2 = _outproj(olat, u, halo_src, x1, pw['w_uv'], pw['w_pool'], pw['pool_scale'], pw['w_o'],
                  *pw['ln2'], alpha=alpha, seq=seq, offset=offset, tm=tmo, prompt=prompt)
    x3 = _ffn_ln(x2, pw['ffn2_gu'], pw['ffn2_down'], *pw['ln3'], alpha=alpha, tm=tm, tf=tf)
    y = _ple(x3, p.reshape(t, -1), pw['w_ple_gate'], pw['w_ple_proj'], *pw['ln4'],
             alpha=alpha, tm=tm)

    u3 = u.reshape(bsz, seq, -1)
    if seq >= POOL_HIST:
        new_hist = u3[:, seq - POOL_HIST:]
    else:
        new_hist = jnp.concatenate([pool_hist, u3], axis=1)[:, -POOL_HIST:]
    return (y.reshape(bsz, seq, d), ckv.reshape(bsz, seq, -1), kr.reshape(bsz, seq, -1), new_hist)


def kernel(x_prompt, x_sample, cache_kv_latent, cache_k_rope, state_pool, p_prompt, p_sample, ffn1_gu, ffn1_down, ln1_g, ln1_b, w_in, g_q, g_kv, w_uq_nope, w_uq_rope, w_uk, w_uv, w_pool, pool_scale, w_o, ln2_g, ln2_b, ffn2_gu, ffn2_down, ln3_g, ln3_b, w_ple_gate, w_ple_proj, ln4_g, ln4_b):
    depth = ffn1_gu.shape[0]
    past_len = cache_kv_latent.shape[2]
    alpha = (2.0 * depth) ** 0.25
    stacked = dict(
        ffn1_gu=ffn1_gu, ffn1_down=ffn1_down, ln1_g=ln1_g, ln1_b=ln1_b, w_in=w_in, g_q=g_q,
        g_kv=g_kv, w_uq_nope=w_uq_nope, w_uq_rope=w_uq_rope, w_uk=w_uk, w_uv=w_uv,
        w_pool=w_pool, pool_scale=pool_scale, w_o=w_o, ln2_g=ln2_g, ln2_b=ln2_b,
        ffn2_gu=ffn2_gu, ffn2_down=ffn2_down, ln3_g=ln3_g, ln3_b=ln3_b,
        w_ple_gate=w_ple_gate, w_ple_proj=w_ple_proj, ln4_g=ln4_g, ln4_b=ln4_b)
    hp, hs = x_prompt, x_sample
    outs = [[] for _ in range(6)]
    for i in range(depth):
        pw = _prep_layer_weights({k: v[i] for k, v in stacked.items()})
        hp, c1, k1, s1 = _encoder_layer(hp, p_prompt[i], None, None, 0, pw, alpha=alpha)
        hs, c2, k2, s2 = _encoder_layer(
            hs, p_sample[i], (cache_kv_latent[i], cache_k_rope[i]), state_pool[i], past_len, pw,
            alpha=alpha)
        for acc, v in zip(outs, (c1, k1, s1, c2, k2, s2)):
            acc.append(v)
    return (hp, hs) + tuple(jnp.stack(v) for v in outs)
```

```python
import functools
import math

import jax
import jax.numpy as jnp
from jax import lax
from jax.experimental import pallas as pl
from jax.experimental.pallas import tpu as pltpu

CHUNK = 64
N_HEADS = 8
QK_NOPE = 128
QK_ROPE = 64
V_HEAD = 128
POOL_WINDOWS = (2, 4, 8, 16)
POOL_HIST = max(POOL_WINDOWS) - 1
HALO = POOL_HIST + 1
ROPE_THETA = 10000.0
LN_EPS = 1e-5
RMS_EPS = 1e-6
ATTN_SCALE = (QK_NOPE + QK_ROPE) ** -0.5
SCORE_SCALE = ATTN_SCALE * math.log2(math.e)

V7X_VMEM_LIMIT_BYTES = 56 * 1024 * 1024
STAT_LANES = 128
SOFTMAX_ROW_PARTS = 4
SAMPLE_HEAD_GROUPS = 2

_F32 = jnp.float32
_BF16 = jnp.bfloat16
_NT_DIMS = (((1,), (1,)), ((), ()))


def _dot(a, b):
    return jnp.dot(a, b, preferred_element_type=_F32)


def _dot_nt(a, b):
    return lax.dot_general(a, b, _NT_DIMS, preferred_element_type=_F32)


def _layer_norm(y, g, b):
    mu = jnp.mean(y, axis=-1, keepdims=True)
    d = y - mu
    var = jnp.mean(d * d, axis=-1, keepdims=True)
    return d * lax.rsqrt(var + LN_EPS) * g + b


def _rms_norm(z, g):
    return z * lax.rsqrt(jnp.mean(z * z, axis=-1, keepdims=True) + RMS_EPS) * g


def _row_groups(rows, group_rows=256):
    group_rows = min(rows, group_rows)
    return [slice(r, r + group_rows) for r in range(0, rows, group_rows)]


def _params(semantics):
    return pltpu.CompilerParams(
        dimension_semantics=semantics, vmem_limit_bytes=V7X_VMEM_LIMIT_BYTES)


def _ffn_ln_kernel(x_ref, wg_ref, wu_ref, wd_ref, g_ref, b_ref, o_ref, xb_ref, *, alpha):
    j = pl.program_id(1)

    @pl.when(j == 0)
    def _():
        xb_ref[...] = x_ref[...].astype(_BF16)
        o_ref[...] = jnp.zeros_like(o_ref)

    xb = xb_ref[...]
    gate = _dot(xb, wg_ref[...])
    up = _dot(xb, wu_ref[...])
    act = (gate * jax.nn.sigmoid(gate) * up).astype(_BF16)
    o_ref[...] += _dot(act, wd_ref[...])

    @pl.when(j == pl.num_programs(1) - 1)
    def _():
        y = alpha * x_ref[...] + 0.5 * o_ref[...]
        o_ref[...] = _layer_norm(y, g_ref[...], b_ref[...])


def _ffn_ln(x, w_gu, w_down, g, b, *, alpha, tm, tf):
    t, d = x.shape
    f = w_down.shape[0]
    nj = f // tf
    return pl.pallas_call(
        functools.partial(_ffn_ln_kernel, alpha=alpha),
        grid=(t // tm, nj),
        in_specs=[
            pl.BlockSpec((tm, d), lambda i, j: (i, 0)),
            pl.BlockSpec((d, tf), lambda i, j: (0, j)),
            pl.BlockSpec((d, tf), lambda i, j: (0, j + nj)),
            pl.BlockSpec((tf, d), lambda i, j: (j, 0)),
            pl.BlockSpec((1, d), lambda i, j: (0, 0)),
            pl.BlockSpec((1, d), lambda i, j: (0, 0)),
        ],
        out_specs=pl.BlockSpec((tm, d), lambda i, j: (i, 0)),
        out_shape=jax.ShapeDtypeStruct((t, d), _F32),
        scratch_shapes=[pltpu.VMEM((tm, d), _BF16)],
        compiler_params=_params(("parallel", "arbitrary")),
        name="ffn_ln",
    )(x, w_gu, w_gu, w_down, g, b)


def _inproj_kernel(x_ref, win_ref, gq_ref, gkv_ref, wq_ref, wuk_ref, cs_ref,
                   *out_refs, q_rank, kv_rank, absorbed):
    if absorbed:
        qlat_ref, qrope_ref, ckv_ref, kr_ref, ckvb_ref, krb_ref, u_ref = out_refs
    else:
        qcat_ref, kcat_ref, ckv_ref, kr_ref, ckvb_ref, u_ref = out_refs
    xb = x_ref[...].astype(_BF16)
    z = _dot(xb, win_ref[...])
    o_kv = q_rank
    o_kr = o_kv + kv_rank
    o_u = o_kr + 2 * QK_ROPE
    u_ref[...] = z[:, o_u:]

    c_kv = _rms_norm(z[:, o_kv:o_kr], gkv_ref[...])
    c_kv_b = c_kv.astype(_BF16)
    ckv_ref[...] = c_kv
    ckvb_ref[...] = c_kv_b

    kt = z[:, o_kr:o_u] * cs_ref[...]
    k2 = kt + pltpu.roll(kt, QK_ROPE, axis=1)
    kr_ref[...] = k2[:, :QK_ROPE]

    c_q = _rms_norm(z[:, :q_rank], gq_ref[...]).astype(_BF16)
    qz = _dot(c_q, wq_ref[...])
    n_nope = N_HEADS * QK_NOPE
    n_rope = N_HEADS * QK_ROPE
    cs = cs_ref[...]
    sc = pltpu.roll(cs, QK_ROPE, axis=1)
    first = lax.broadcasted_iota(jnp.int32, cs.shape, 1) < QK_ROPE
    lane_reps = n_rope // cs.shape[1]
    cos = jnp.concatenate([jnp.where(first, cs, sc)] * lane_reps, axis=1)
    sin = jnp.concatenate([jnp.where(first, sc, cs)] * lane_reps, axis=1)
    q_rope = (qz[:, n_nope:n_nope + n_rope] * cos + qz[:, n_nope + n_rope:] * sin) * SCORE_SCALE
    if absorbed:
        krb_ref[...] = k2[:, :QK_ROPE].astype(_BF16)
        q_rope = q_rope.astype(_BF16)
        q_nope = qz[:, :n_nope].astype(_BF16)
        for h in range(N_HEADS):
            hs = slice(h * QK_NOPE, (h + 1) * QK_NOPE)
            q_lat = _dot_nt(q_nope[:, hs], wuk_ref[:, hs]) * SCORE_SCALE
            qlat_ref[h] = q_lat.astype(_BF16)
            qrope_ref[h] = q_rope[:, h * QK_ROPE:(h + 1) * QK_ROPE]
    else:
        pair = 2 * QK_ROPE
        low = lax.broadcasted_iota(jnp.int32, (x_ref.shape[0], pair), 1) < QK_ROPE
        k_tail = jnp.where(low, k2, 0.0).astype(_BF16)
        q_nope = (qz[:, :n_nope] * SCORE_SCALE).astype(_BF16)
        k_nope = _dot(c_kv_b, wuk_ref[...]).astype(_BF16)
        for h in range(N_HEADS):
            v = q_rope[:, (h // 2) * pair:(h // 2 + 1) * pair]
            if h % 2:
                v = pltpu.roll(v, QK_ROPE, axis=1)
            q_tail = jnp.where(low, v, 0.0).astype(_BF16)
            hs = slice(h * QK_NOPE, (h + 1) * QK_NOPE)
            qcat_ref[h] = jnp.concatenate([q_nope[:, hs], q_tail], axis=1)
            kcat_ref[h] = jnp.concatenate([k_nope[:, hs], k_tail], axis=1)


def _inproj(x, w_in, g_q, g_kv, w_q, w_uk, cs, *, tm, absorbed):
    t, d = x.shape
    q_rank = g_q.shape[1]
    kv_rank = g_kv.shape[1]
    pool_width = w_in.shape[1] - q_rank - kv_rank - 2 * QK_ROPE
    nper = cs.shape[0] // tm
    const2 = lambda i: (0, 0)
    row = lambda i: (i, 0)
    tab = lambda i: (i % nper, 0)
    head_row = lambda i: (0, i, 0)

    def rows_out(width, dtype):
        return pl.BlockSpec((tm, width), row), jax.ShapeDtypeStruct((t, width), dtype)

    def heads_out(width):
        return (pl.BlockSpec((N_HEADS, tm, width), head_row),
                jax.ShapeDtypeStruct((N_HEADS, t, width), _BF16))

    cat = QK_NOPE + 2 * QK_ROPE
    if absorbed:
        outs = [heads_out(kv_rank), heads_out(QK_ROPE)]
    else:
        outs = [heads_out(cat), heads_out(cat)]
    outs += [rows_out(kv_rank, _F32), rows_out(QK_ROPE, _F32), rows_out(kv_rank, _BF16)]
    if absorbed:
        outs.append(rows_out(QK_ROPE, _BF16))
    outs.append(rows_out(pool_width, _F32))
    return pl.pallas_call(
        functools.partial(_inproj_kernel, q_rank=q_rank, kv_rank=kv_rank, absorbed=absorbed),
        grid=(t // tm,),
        in_specs=[
            pl.BlockSpec((tm, d), row),
            pl.BlockSpec(w_in.shape, const2),
            pl.BlockSpec((1, q_rank), const2),
            pl.BlockSpec((1, kv_rank), const2),
            pl.BlockSpec(w_q.shape, const2),
            pl.BlockSpec(w_uk.shape, const2),
            pl.BlockSpec((tm, cs.shape[1]), tab),
        ],
        out_specs=[o[0] for o in outs],
        out_shape=[o[1] for o in outs],
        compiler_params=_params(("parallel",)),
        name="inproj",
    )(x, w_in, g_q, g_kv, w_q, w_uk, cs)


def _attn_prompt_kernel(q_ref, k_ref, v_ref, o_ref, s0_ref, s1_ref, p0_ref, p1_ref, a0_ref, a1_ref,
                        m_ref, l_ref, acc_ref, *, tq, tk):
    i = pl.program_id(2)
    heads = q_ref.shape[0]
    m_ref[...] = jnp.full_like(m_ref, -jnp.inf)
    l_ref[...] = jnp.zeros_like(l_ref)
    acc_ref[...] = jnp.zeros_like(acc_ref)
    s_bufs = (s0_ref, s1_ref)
    p_bufs = ((p0_ref, a0_ref), (p1_ref, a1_ref))

    def keys(j):
        return pl.ds(pl.multiple_of(j * tk, tk), tk)

    def lanes(v, width):
        return jnp.concatenate([v] * (width // v.shape[1]), axis=1)

    def row_stat(v):
        return jnp.broadcast_to(v, (v.shape[0], STAT_LANES))

    def scores(j, b):
        for h in range(heads):
            s_bufs[b][h] = _dot_nt(q_ref[h], k_ref[h, keys(j), :])

    def softmax(j, b, masked=False):
        p_ref, a_ref = p_bufs[b]
        part = tq // SOFTMAX_ROW_PARTS
        for h in range(heads):
            for r in range(SOFTMAX_ROW_PARTS):
                rs = slice(r * part, (r + 1) * part)
                s = s_bufs[b][h, rs, :]
                if masked:
                    q_tok = lax.broadcasted_iota(jnp.int32, (part, 1), 0) + i * tq + r * part
                    k_tok = lax.broadcasted_iota(jnp.int32, (1, tk), 1) + j * tk
                    s = jnp.where(k_tok // CHUNK <= q_tok // CHUNK, s, -jnp.inf)
                m_old = m_ref[h, rs, :]
                m_new = jnp.maximum(m_old, row_stat(jnp.max(s, axis=1, keepdims=True)))
                p = jnp.exp2(s - lanes(m_new, tk))
                a = jnp.exp2(m_old - m_new)
                l_ref[h, rs, :] = a * l_ref[h, rs, :] + row_stat(jnp.sum(p, axis=1, keepdims=True))
                m_ref[h, rs, :] = m_new
                a_ref[h, rs, :] = a
                ps = slice(r * part // 2, (r + 1) * part // 2)
                p_ref[h, ps, :] = pltpu.bitcast(p.astype(_BF16), jnp.uint32)

    def values(j, b):
        p_ref, a_ref = p_bufs[b]
        for h in range(heads):
            acc_ref[h] = (lanes(a_ref[h], acc_ref.shape[2]) * acc_ref[h]
                          + _dot(pltpu.bitcast(p_ref[h], _BF16), v_ref[keys(j), :]))

    n_full = (i * tq) // tk
    scores(0, 0)

    @pl.when(n_full == 0)
    def _():
        softmax(0, 0, masked=True)
        values(0, 0)

    @pl.when(n_full > 0)
    def _():
        scores(1, 1)
        softmax(0, 0)

        def pair(k, carry):
            t = 2 * k + 2
            scores(t, 0)
            softmax(t - 1, 1)
            values(t - 2, 0)
            scores(t + 1, 1)
            softmax(t, 0)
            values(t - 1, 1)
            return carry

        lax.fori_loop(0, (n_full - 1) // 2, pair, 0)

        @pl.when(n_full % 2 == 1)
        def _():
            softmax(n_full, 1, masked=True)
            values(n_full - 1, 0)
            values(n_full, 1)

        @pl.when(n_full % 2 == 0)
        def _():
            scores(n_full, 0)
            softmax(n_full - 1, 1)
            values(n_full - 2, 0)
            softmax(n_full, 0, masked=True)
            values(n_full - 1, 1)
            values(n_full, 0)

    for h in range(heads):
        o_ref[h] = (acc_ref[h] / lanes(l_ref[h], acc_ref.shape[2])).astype(o_ref.dtype)


def _attn_prompt(qcat, kcat, ckvb, *, batch, seq, tq, tk, heads):
    assert tk % tq == 0 and tq % CHUNK == 0 and seq % tk == 0
    h, t, e = qcat.shape
    c = ckvb.shape[1]
    nq = seq // tq
    q_block = lambda b, hg, i: (hg, b * nq + i, 0)
    stat = pltpu.VMEM((heads, tq, STAT_LANES), _F32)
    tile = pltpu.VMEM((heads, tq, tk), _F32)
    packed = pltpu.VMEM((heads, tq // 2, tk), jnp.uint32)
    return pl.pallas_call(
        functools.partial(_attn_prompt_kernel, tq=tq, tk=tk),
        grid=(batch, h // heads, nq),
        in_specs=[
            pl.BlockSpec((heads, tq, e), q_block),
            pl.BlockSpec((heads, seq, e), lambda b, hg, i: (hg, b, 0)),
            pl.BlockSpec((seq, c), lambda b, hg, i: (b, 0)),
        ],
        out_specs=pl.BlockSpec((heads, tq, c), q_block),
        out_shape=jax.ShapeDtypeStruct((h, t, c), _BF16),
        scratch_shapes=[
            tile, tile,
            packed, packed,
            stat, stat,
            stat,
            stat,
            pltpu.VMEM((heads, tq, c), _F32),
        ],
        compiler_params=_params(("parallel", "parallel", "arbitrary")),
        name="attn_prompt",
    )(qcat, kcat, ckvb)


def _attn_sample_kernel(ql_ref, qr_ref, pkv_ref, pkr_ref, kv_ref, kr_ref, o_ref):
    pkv = pkv_ref[0].astype(_BF16)
    pkr = pkr_ref[0].astype(_BF16)
    kv = kv_ref[...]
    kr = kr_ref[...]
    hg = ql_ref.shape[0] // SAMPLE_HEAD_GROUPS
    rows = hg * ql_ref.shape[1]
    for g in range(SAMPLE_HEAD_GROUPS):
        hs = slice(g * hg, (g + 1) * hg)
        ql = ql_ref[hs].reshape(rows, ql_ref.shape[2])
        qr = qr_ref[hs].reshape(rows, qr_ref.shape[2])
        s_past = _dot_nt(ql, pkv) + _dot_nt(qr, pkr)
        s_new = _dot_nt(ql, kv) + _dot_nt(qr, kr)
        m = jnp.maximum(jnp.max(s_past, axis=1, keepdims=True),
                        jnp.max(s_new, axis=1, keepdims=True))
        p_past = jnp.exp2(s_past - m)
        p_new = jnp.exp2(s_new - m)
        l = jnp.sum(p_past, axis=1, keepdims=True) + jnp.sum(p_new, axis=1, keepdims=True)
        o = (_dot(p_past.astype(_BF16), pkv) + _dot(p_new.astype(_BF16), kv)) / l
        o_ref[hs] = o.reshape((hg,) + o_ref.shape[1:]).astype(o_ref.dtype)


def _attn_sample(qlat, qrope, past_kv, past_kr, ckvb, krb, *, seq):
    h, t, c = qlat.shape
    r = qrope.shape[2]
    batch, past_len, _ = past_kv.shape
    head_row = lambda b: (0, b, 0)
    return pl.pallas_call(
        _attn_sample_kernel,
        grid=(batch,),
        in_specs=[
            pl.BlockSpec((h, seq, c), head_row),
            pl.BlockSpec((h, seq, r), head_row),
            pl.BlockSpec((1, past_len, c), lambda b: (b, 0, 0)),
            pl.BlockSpec((1, past_len, r), lambda b: (b, 0, 0)),
            pl.BlockSpec((seq, c), lambda b: (b, 0)),
            pl.BlockSpec((seq, r), lambda b: (b, 0)),
        ],
        out_specs=pl.BlockSpec((h, seq, c), head_row),
        out_shape=jax.ShapeDtypeStruct((h, t, c), _BF16),
        compiler_params=_params(("parallel",)),
        name="attn_sample",
    )(qlat, qrope, past_kv, past_kr, ckvb, krb)


def _pool_delta(ext, pos0, group):
    n = ext.shape[0] - HALO
    pos = pos0 + lax.broadcasted_iota(jnp.int32, (n, 1), 0)
    out = []
    for gi, w in enumerate(POOL_WINDOWS):
        cols = ext[:, gi * group:(gi + 1) * group]
        s = cols
        shift = 1
        while shift < w:
            s = s + pltpu.roll(s, shift, axis=0)
            shift *= 2
        cnt = jnp.minimum(w, pos + 1).astype(_F32)
        out.append(s[HALO:] / cnt - cols[HALO:])
    return out


def _outproj_kernel(olat_ref, u_ref, halo_ref, x_ref, wuv_ref, wpool_ref, pscale_ref, wo_ref,
                    g_ref, b_ref, o_ref, *, alpha, seq, offset, blocks_per_seq):
    tm = u_ref.shape[0]
    group = u_ref.shape[1] // len(POOL_WINDOWS)
    u = u_ref[...]
    halo = halo_ref[...]
    if blocks_per_seq is not None:
        bi = pl.program_id(0) % blocks_per_seq
        halo = jnp.where(bi == 0, 0.0, halo)
        deltas = [_pool_delta(jnp.concatenate([halo, u], axis=0), offset + bi * tm, group)]
    else:
        deltas = []
        for s in range(tm // seq):
            ext = jnp.concatenate(
                [halo[s * HALO:(s + 1) * HALO], u[s * seq:(s + 1) * seq]], axis=0)
            deltas.append(_pool_delta(ext, offset, group))
    pooled = []
    for gi in range(len(POOL_WINDOWS)):
        d = jnp.concatenate([ds[gi] for ds in deltas], axis=0) if len(deltas) > 1 else deltas[0][gi]
        pooled.append(_dot(d.astype(_BF16), wpool_ref[gi]))
    o_pool = jnp.concatenate(pooled, axis=1) * pscale_ref[...]
    o_attn = jnp.concatenate(
        [_dot(olat_ref[h], wuv_ref[:, h * V_HEAD:(h + 1) * V_HEAD]) for h in range(N_HEADS)], axis=1)
    mixed = jnp.concatenate([o_attn, o_pool], axis=1).astype(_BF16)
    for rs in _row_groups(tm):
        y = alpha * x_ref[rs, :] + _dot(mixed[rs], wo_ref[...])
        o_ref[rs, :] = _layer_norm(y, g_ref[...], b_ref[...])


def _outproj(olat, u, halo_src, x, w_uv, w_pool, pool_scale, w_o, g, b,
             *, alpha, seq, offset, tm, prompt):
    t, d = x.shape
    h, _, c = olat.shape
    pw = u.shape[1]
    row = lambda i: (i, 0)
    const2 = lambda i: (0, 0)
    const3 = lambda i: (0, 0, 0)
    if prompt:
        blocks_per_seq = seq // tm
        step = tm // HALO
        halo_spec = pl.BlockSpec((HALO, pw), lambda i: (jnp.maximum(i * step - 1, 0), 0))
    else:
        blocks_per_seq = None
        halo_spec = pl.BlockSpec(((tm // seq) * HALO, pw), row)
    return pl.pallas_call(
        functools.partial(_outproj_kernel, alpha=alpha, seq=seq, offset=offset,
                          blocks_per_seq=blocks_per_seq),
        grid=(t // tm,),
        in_specs=[
            pl.BlockSpec((h, tm, c), lambda i: (0, i, 0)),
            pl.BlockSpec((tm, pw), row),
            halo_spec,
            pl.BlockSpec((tm, d), row),
            pl.BlockSpec(w_uv.shape, const2),
            pl.BlockSpec(w_pool.shape, const3),
            pl.BlockSpec((1, pw), const2),
            pl.BlockSpec(w_o.shape, const2),
            pl.BlockSpec((1, d), const2),
            pl.BlockSpec((1, d), const2),
        ],
        out_specs=pl.BlockSpec((tm, d), row),
        out_shape=jax.ShapeDtypeStruct((t, d), _F32),
        compiler_params=_params(("parallel",)),
        name="outproj_prompt" if prompt else "outproj_sample",
    )(olat, u, halo_src, x, w_uv, w_pool, pool_scale, w_o, g, b)


def _ple_kernel(x_ref, p_ref, wg_ref, wp_ref, g_ref, b_ref, o_ref, *, alpha):
    for rs in _row_groups(x_ref.shape[0]):
        x = x_ref[rs, :]
        gate = jax.nn.sigmoid(_dot(x.astype(_BF16), wg_ref[...]))
        proj = _dot(p_ref[rs, :].astype(_BF16), wp_ref[...])
        o_ref[rs, :] = _layer_norm(alpha * x + gate * proj, g_ref[...], b_ref[...])


def _ple(x, p, w_gate, w_proj, g, b, *, alpha, tm):
    t, d = x.shape
    pd = p.shape[1]
    row = lambda i: (i, 0)
    const2 = lambda i: (0, 0)
    return pl.pallas_call(
        functools.partial(_ple_kernel, alpha=alpha),
        grid=(t // tm,),
        in_specs=[
            pl.BlockSpec((tm, d), row),
            pl.BlockSpec((tm, pd), row),
            pl.BlockSpec(w_gate.shape, const2),
            pl.BlockSpec(w_proj.shape, const2),
            pl.BlockSpec((1, d), const2),
            pl.BlockSpec((1, d), const2),
        ],
        out_specs=pl.BlockSpec((tm, d), row),
        out_shape=jax.ShapeDtypeStruct((t, d), _F32),
        compiler_params=_params(("parallel",)),
        name="ple",
    )(x, p, w_gate, w_proj, g, b)


def _rotate_half_cols(w):
    half = w.shape[-1] // 2
    return jnp.concatenate([-w[..., half:], w[..., :half]], axis=-1)


def _rope_table(pos, reps):
    inv = 1.0 / (ROPE_THETA ** (jnp.arange(0, QK_ROPE, 2, dtype=_F32) / QK_ROPE))
    ang = pos.astype(_F32)[:, None] * inv[None, :]
    ang = jnp.concatenate([ang, ang], -1)
    return jnp.tile(jnp.concatenate([jnp.cos(ang), jnp.sin(ang)], -1), (reps, 1))


def _tile_rows(total, preferred):
    tile = min(total, preferred)
    assert total % tile == 0
    return tile


def _prep_layer_weights(w):
    q_rank = w['g_q'].shape[0]
    kv_rank = w['g_kv'].shape[0]
    w_in = w['w_in']
    o_kr = q_rank + kv_rank
    w_kr = w_in[:, o_kr:o_kr + QK_ROPE]
    w_in_all = jnp.concatenate(
        [w_in[:, :o_kr], w_kr, _rotate_half_cols(w_kr), w_in[:, o_kr + QK_ROPE:]], axis=1)
    w_q = jnp.concatenate(
        [w['w_uq_nope'].reshape(q_rank, -1), w['w_uq_rope'].reshape(q_rank, -1),
         _rotate_half_cols(w['w_uq_rope']).reshape(q_rank, -1)], axis=1)
    row = lambda v: v.reshape(1, -1)
    return dict(
        ffn1_gu=w['ffn1_gu'].astype(_BF16), ffn1_down=w['ffn1_down'].astype(_BF16),
        ffn2_gu=w['ffn2_gu'].astype(_BF16), ffn2_down=w['ffn2_down'].astype(_BF16),
        w_in=w_in_all.astype(_BF16), w_q=w_q.astype(_BF16),
        w_uk=w['w_uk'].reshape(kv_rank, -1).astype(_BF16),
        w_uv=w['w_uv'].reshape(kv_rank, -1).astype(_BF16),
        w_pool=w['w_pool'].astype(_BF16), pool_scale=row(w['pool_scale']),
        w_o=w['w_o'].astype(_BF16),
        w_ple_gate=w['w_ple_gate'].astype(_BF16), w_ple_proj=w['w_ple_proj'].astype(_BF16),
        g_q=row(w['g_q']), g_kv=row(w['g_kv']),
        ln1=(row(w['ln1_g']), row(w['ln1_b'])), ln2=(row(w['ln2_g']), row(w['ln2_b'])),
        ln3=(row(w['ln3_g']), row(w['ln3_b'])), ln4=(row(w['ln4_g']), row(w['ln4_b'])),
    )


def _encoder_layer(x, p, past, pool_hist, offset, pw, *, alpha):
    bsz, seq, d = x.shape
    t = bsz * seq
    prompt = past is None
    xf = x.reshape(t, d)
    tm = _tile_rows(t, 512)
    d_ff = pw['ffn1_down'].shape[0]
    tf = 512 if d_ff % 512 == 0 else d_ff

    x1 = _ffn_ln(xf, pw['ffn1_gu'], pw['ffn1_down'], *pw['ln1'], alpha=alpha, tm=tm, tf=tf)

    tmi = _tile_rows(t, 256)
    if prompt:
        assert seq % tmi == 0
        tab = _rope_table(offset + jnp.arange(seq), 1)
    else:
        assert tmi % seq == 0
        tab = _rope_table(offset + jnp.arange(seq), tmi // seq)
    proj = _inproj(x1, pw['w_in'], pw['g_q'], pw['g_kv'], pw['w_q'], pw['w_uk'], tab, tm=tmi,
                   absorbed=not prompt)
    if prompt:
        qcat, kcat, ckv, kr, ckvb, u = proj
        tq = tk = min(seq, 512)
        olat = _attn_prompt(qcat, kcat, ckvb, batch=bsz, seq=seq, tq=tq, tk=tk, heads=2)
        halo_src = u
    else:
        qlat, qrope, ckv, kr, ckvb, krb, u = proj
        olat = _attn_sample(qlat, qrope, past[0], past[1], ckvb, krb, seq=seq)
        halo_src = jnp.pad(pool_hist, ((0, 0), (HALO - POOL_HIST, 0), (0, 0))).reshape(bsz * HALO, -1)

    tmo = _tile_rows(t, 512 if prompt else 256)
    if prompt:
        assert seq % tmo == 0 and tmo % HALO == 0
    else:
        assert tmo % seq == 0
    x2 = _outproj(olat, u, halo_src, x1, pw['w_uv'], pw['w_pool'], pw['pool_scale'], pw['w_o'],
                  *pw['ln2'], alpha=alpha, seq=seq, offset=offset, tm=tmo, prompt=prompt)
    x3 = _ffn_ln(x2, pw['ffn2_gu'], pw['ffn2_down'], *pw['ln3'], alpha=alpha, tm=tm, tf=tf)
    y = _ple(x3, p.reshape(t, -1), pw['w_ple_gate'], pw['w_ple_proj'], *pw['ln4'],
             alpha=alpha, tm=tm)

    u3 = u.reshape(bsz, seq, -1)
    if seq >= POOL_HIST:
        new_hist = u3[:, seq - POOL_HIST:]
    else:
        new_hist = jnp.concatenate([pool_hist, u3], axis=1)[:, -POOL_HIST:]
    return (y.reshape(bsz, seq, d), ckv.reshape(bsz, seq, -1), kr.reshape(bsz, seq, -1), new_hist)


def kernel(x_prompt, x_sample, cache_kv_latent, cache_k_rope, state_pool, p_prompt, p_sample, ffn1_gu, ffn1_down, ln1_g, ln1_b, w_in, g_q, g_kv, w_uq_nope, w_uq_rope, w_uk, w_uv, w_pool, pool_scale, w_o, ln2_g, ln2_b, ffn2_gu, ffn2_down, ln3_g, ln3_b, w_ple_gate, w_ple_proj, ln4_g, ln4_b):
    depth = ffn1_gu.shape[0]
    past_len = cache_kv_latent.shape[2]
    alpha = (2.0 * depth) ** 0.25
    stacked = dict(
        ffn1_gu=ffn1_gu, ffn1_down=ffn1_down, ln1_g=ln1_g, ln1_b=ln1_b, w_in=w_in, g_q=g_q,
        g_kv=g_kv, w_uq_nope=w_uq_nope, w_uq_rope=w_uq_rope, w_uk=w_uk, w_uv=w_uv,
        w_pool=w_pool, pool_scale=pool_scale, w_o=w_o, ln2_g=ln2_g, ln2_b=ln2_b,
        ffn2_gu=ffn2_gu, ffn2_down=ffn2_down, ln3_g=ln3_g, ln3_b=ln3_b,
        w_ple_gate=w_ple_gate, w_ple_proj=w_ple_proj, ln4_g=ln4_g, ln4_b=ln4_b)
    hp, hs = x_prompt, x_sample
    outs = [[] for _ in range(6)]
    for i in range(depth):
        pw = _prep_layer_weights({k: v[i] for k, v in stacked.items()})
        hp, c1, k1, s1 = _encoder_layer(hp, p_prompt[i], None, None, 0, pw, alpha=alpha)
        hs, c2, k2, s2 = _encoder_layer(
            hs, p_sample[i], (cache_kv_latent[i], cache_k_rope[i]), state_pool[i], past_len, pw,
            alpha=alpha)
        for acc, v in zip(outs, (c1, k1, s1, c2, k2, s2)):
            acc.append(v)
    return (hp, hs) + tuple(jnp.stack(v) for v in outs)
```

```python
import functools
import math

import jax
import jax.numpy as jnp
from jax import lax
from jax.experimental import pallas as pl
from jax.experimental.pallas import tpu as pltpu

CHUNK = 64
N_HEADS = 8
QK_NOPE = 128
QK_ROPE = 64
V_HEAD = 128
POOL_WINDOWS = (2, 4, 8, 16)
POOL_HIST = max(POOL_WINDOWS) - 1
HALO = POOL_HIST + 1
ROPE_THETA = 10000.0
LN_EPS = 1e-5
RMS_EPS = 1e-6
ATTN_SCALE = (QK_NOPE + QK_ROPE) ** -0.5
SCORE_SCALE = ATTN_SCALE * math.log2(math.e)

V7X_VMEM_LIMIT_BYTES = 56 * 1024 * 1024
STAT_LANES = 128
SOFTMAX_ROW_PARTS = 4
SAMPLE_HEAD_GROUPS = 2

_F32 = jnp.float32
_BF16 = jnp.bfloat16
_NT_DIMS = (((1,), (1,)), ((), ()))


def _dot(a, b):
    return jnp.dot(a, b, preferred_element_type=_F32)


def _dot_nt(a, b):
    return lax.dot_general(a, b, _NT_DIMS, preferred_element_type=_F32)


def _layer_norm(y, g, b):
    mu = jnp.mean(y, axis=-1, keepdims=True)
    d = y - mu
    var = jnp.mean(d * d, axis=-1, keepdims=True)
    return d * lax.rsqrt(var + LN_EPS) * g + b


def _rms_norm(z, g):
    return z * lax.rsqrt(jnp.mean(z * z, axis=-1, keepdims=True) + RMS_EPS) * g


def _row_groups(rows, group_rows=256):
    group_rows = min(rows, group_rows)
    return [slice(r, r + group_rows) for r in range(0, rows, group_rows)]


def _params(semantics):
    return pltpu.CompilerParams(
        dimension_semantics=semantics, vmem_limit_bytes=V7X_VMEM_LIMIT_BYTES)


def _ffn_ln_kernel(x_ref, wg_ref, wu_ref, wd_ref, g_ref, b_ref, o_ref, xb_ref, *, alpha):
    j = pl.program_id(1)

    @pl.when(j == 0)
    def _():
        xb_ref[...] = x_ref[...].astype(_BF16)
        o_ref[...] = jnp.zeros_like(o_ref)

    xb = xb_ref[...]
    gate = _dot(xb, wg_ref[...])
    up = _dot(xb, wu_ref[...])
    act = (gate * jax.nn.sigmoid(gate) * up).astype(_BF16)
    o_ref[...] += _dot(act, wd_ref[...])

    @pl.when(j == pl.num_programs(1) - 1)
    def _():
        y = alpha * x_ref[...] + 0.5 * o_ref[...]
        o_ref[...] = _layer_norm(y, g_ref[...], b_ref[...])


def _ffn_ln(x, w_gu, w_down, g, b, *, alpha, tm, tf):
    t, d = x.shape
    f = w_down.shape[0]
    nj = f // tf
    return pl.pallas_call(
        functools.partial(_ffn_ln_kernel, alpha=alpha),
        grid=(t // tm, nj),
        in_specs=[
            pl.BlockSpec((tm, d), lambda i, j: (i, 0)),
            pl.BlockSpec((d, tf), lambda i, j: (0, j)),
            pl.BlockSpec((d, tf), lambda i, j: (0, j + nj)),
            pl.BlockSpec((tf, d), lambda i, j: (j, 0)),
            pl.BlockSpec((1, d), lambda i, j: (0, 0)),
            pl.BlockSpec((1, d), lambda i, j: (0, 0)),
        ],
        out_specs=pl.BlockSpec((tm, d), lambda i, j: (i, 0)),
        out_shape=jax.ShapeDtypeStruct((t, d), _F32),
        scratch_shapes=[pltpu.VMEM((tm, d), _BF16)],
        compiler_params=_params(("parallel", "arbitrary")),
        name="ffn_ln",
    )(x, w_gu, w_gu, w_down, g, b)


def _inproj_kernel(x_ref, win_ref, gq_ref, gkv_ref, wq_ref, wuk_ref, cs_ref,
                   *out_refs, q_rank, kv_rank, absorbed):
    if absorbed:
        qlat_ref, qrope_ref, ckv_ref, kr_ref, ckvb_ref, krb_ref, u_ref = out_refs
    else:
        qcat_ref, kcat_ref, ckv_ref, kr_ref, ckvb_ref, u_ref = out_refs
    xb = x_ref[...].astype(_BF16)
    z = _dot(xb, win_ref[...])
    o_kv = q_rank
    o_kr = o_kv + kv_rank
    o_u = o_kr + 2 * QK_ROPE
    u_ref[...] = z[:, o_u:]

    c_kv = _rms_norm(z[:, o_kv:o_kr], gkv_ref[...])
    c_kv_b = c_kv.astype(_BF16)
    ckv_ref[...] = c_kv
    ckvb_ref[...] = c_kv_b

    kt = z[:, o_kr:o_u] * cs_ref[...]
    k2 = kt + pltpu.roll(kt, QK_ROPE, axis=1)
    kr_ref[...] = k2[:, :QK_ROPE]

    c_q = _rms_norm(z[:, :q_rank], gq_ref[...]).astype(_BF16)
    qz = _dot(c_q, wq_ref[...])
    n_nope = N_HEADS * QK_NOPE
    n_rope = N_HEADS * QK_ROPE
    cs = cs_ref[...]
    sc = pltpu.roll(cs, QK_ROPE, axis=1)
    first = lax.broadcasted_iota(jnp.int32, cs.shape, 1) < QK_ROPE
    lane_reps = n_rope // cs.shape[1]
    cos = jnp.concatenate([jnp.where(first, cs, sc)] * lane_reps, axis=1)
    sin = jnp.concatenate([jnp.where(first, sc, cs)] * lane_reps, axis=1)
    q_rope = (qz[:, n_nope:n_nope + n_rope] * cos + qz[:, n_nope + n_rope:] * sin) * SCORE_SCALE
    if absorbed:
        krb_ref[...] = k2[:, :QK_ROPE].astype(_BF16)
        q_rope = q_rope.astype(_BF16)
        q_nope = qz[:, :n_nope].astype(_BF16)
        for h in range(N_HEADS):
            hs = slice(h * QK_NOPE, (h + 1) * QK_NOPE)
            q_lat = _dot_nt(q_nope[:, hs], wuk_ref[:, hs]) * SCORE_SCALE
            qlat_ref[h] = q_lat.astype(_BF16)
            qrope_ref[h] = q_rope[:, h * QK_ROPE:(h + 1) * QK_ROPE]
    else:
        pair = 2 * QK_ROPE
        low = lax.broadcasted_iota(jnp.int32, (x_ref.shape[0], pair), 1) < QK_ROPE
        k_tail = jnp.where(low, k2, 0.0).astype(_BF16)
        q_nope = (qz[:, :n_nope] * SCORE_SCALE).astype(_BF16)
        k_nope = _dot(c_kv_b, wuk_ref[...]).astype(_BF16)
        for h in range(N_HEADS):
            v = q_rope[:, (h // 2) * pair:(h // 2 + 1) * pair]
            if h % 2:
                v = pltpu.roll(v, QK_ROPE, axis=1)
            q_tail = jnp.where(low, v, 0.0).astype(_BF16)
            hs = slice(h * QK_NOPE, (h + 1) * QK_NOPE)
            qcat_ref[h] = jnp.concatenate([q_nope[:, hs], q_tail], axis=1)
            kcat_ref[h] = jnp.concatenate([k_nope[:, hs], k_tail], axis=1)


def _inproj(x, w_in, g_q, g_kv, w_q, w_uk, cs, *, tm, absorbed):
    t, d = x.shape
    q_rank = g_q.shape[1]
    kv_rank = g_kv.shape[1]
    pool_width = w_in.shape[1] - q_rank - kv_rank - 2 * QK_ROPE
    nper = cs.shape[0] // tm
    const2 = lambda i: (0, 0)
    row = lambda i: (i, 0)
    tab = lambda i: (i % nper, 0)
    head_row = lambda i: (0, i, 0)

    def rows_out(width, dtype):
        return pl.BlockSpec((tm, width), row), jax.ShapeDtypeStruct((t, width), dtype)

    def heads_out(width):
        return (pl.BlockSpec((N_HEADS, tm, width), head_row),
                jax.ShapeDtypeStruct((N_HEADS, t, width), _BF16))

    cat = QK_NOPE + 2 * QK_ROPE
    if absorbed:
        outs = [heads_out(kv_rank), heads_out(QK_ROPE)]
    else:
        outs = [heads_out(cat), heads_out(cat)]
    outs += [rows_out(kv_rank, _F32), rows_out(QK_ROPE, _F32), rows_out(kv_rank, _BF16)]
    if absorbed:
        outs.append(rows_out(QK_ROPE, _BF16))
    outs.append(rows_out(pool_width, _F32))
    return pl.pallas_call(
        functools.partial(_inproj_kernel, q_rank=q_rank, kv_rank=kv_rank, absorbed=absorbed),
        grid=(t // tm,),
        in_specs=[
            pl.BlockSpec((tm, d), row),
            pl.BlockSpec(w_in.shape, const2),
            pl.BlockSpec((1, q_rank), const2),
            pl.BlockSpec((1, kv_rank), const2),
            pl.BlockSpec(w_q.shape, const2),
            pl.BlockSpec(w_uk.shape, const2),
            pl.BlockSpec((tm, cs.shape[1]), tab),
        ],
        out_specs=[o[0] for o in outs],
        out_shape=[o[1] for o in outs],
        compiler_params=_params(("parallel",)),
        name="inproj",
    )(x, w_in, g_q, g_kv, w_q, w_uk, cs)


def _attn_prompt_kernel(q_ref, k_ref, v_ref, wuv_ref, o_ref, s0_ref, s1_ref, p0_ref, p1_ref, a0_ref, a1_ref,
                        m_ref, l_ref, acc_ref, *, tq, tk):
    i = pl.program_id(2)
    heads = q_ref.shape[0]
    m_ref[...] = jnp.full_like(m_ref, -jnp.inf)
    l_ref[...] = jnp.zeros_like(l_ref)
    acc_ref[...] = jnp.zeros_like(acc_ref)
    s_bufs = (s0_ref, s1_ref)
    p_bufs = ((p0_ref, a0_ref), (p1_ref, a1_ref))

    def keys(j):
        return pl.ds(pl.multiple_of(j * tk, tk), tk)

    def lanes(v, width):
        return jnp.concatenate([v] * (width // v.shape[1]), axis=1)

    def row_stat(v):
        return jnp.broadcast_to(v, (v.shape[0], STAT_LANES))

    def scores(j, b):
        for h in range(heads):
            s_bufs[b][h] = _dot_nt(q_ref[h], k_ref[h, keys(j), :])

    def softmax(j, b, masked=False):
        p_ref, a_ref = p_bufs[b]
        part = tq // SOFTMAX_ROW_PARTS
        for h in range(heads):
            for r in range(SOFTMAX_ROW_PARTS):
                rs = slice(r * part, (r + 1) * part)
                s = s_bufs[b][h, rs, :]
                if masked:
                    q_tok = lax.broadcasted_iota(jnp.int32, (part, 1), 0) + i * tq + r * part
                    k_tok = lax.broadcasted_iota(jnp.int32, (1, tk), 1) + j * tk
                    s = jnp.where(k_tok // CHUNK <= q_tok // CHUNK, s, -jnp.inf)
                m_old = m_ref[h, rs, :]
                m_new = jnp.maximum(m_old, row_stat(jnp.max(s, axis=1, keepdims=True)))
                p = jnp.exp2(s - lanes(m_new, tk))
                a = jnp.exp2(m_old - m_new)
                l_ref[h, rs, :] = a * l_ref[h, rs, :] + row_stat(jnp.sum(p, axis=1, keepdims=True))
                m_ref[h, rs, :] = m_new
                a_ref[h, rs, :] = a
                ps = slice(r * part // 2, (r + 1) * part // 2)
                p_ref[h, ps, :] = pltpu.bitcast(p.astype(_BF16), jnp.uint32)

    def values(j, b):
        p_ref, a_ref = p_bufs[b]
        for h in range(heads):
            acc_ref[h] = (lanes(a_ref[h], acc_ref.shape[2]) * acc_ref[h]
                          + _dot(pltpu.bitcast(p_ref[h], _BF16), v_ref[keys(j), :]))

    n_full = (i * tq) // tk
    scores(0, 0)

    @pl.when(n_full == 0)
    def _():
        softmax(0, 0, masked=True)
        values(0, 0)

    @pl.when(n_full > 0)
    def _():
        scores(1, 1)
        softmax(0, 0)

        def pair(k, carry):
            t = 2 * k + 2
            scores(t, 0)
            softmax(t - 1, 1)
            values(t - 2, 0)
            scores(t + 1, 1)
            softmax(t, 0)
            values(t - 1, 1)
            return carry

        lax.fori_loop(0, (n_full - 1) // 2, pair, 0)

        @pl.when(n_full % 2 == 1)
        def _():
            softmax(n_full, 1, masked=True)
            values(n_full - 1, 0)
            values(n_full, 1)

        @pl.when(n_full % 2 == 0)
        def _():
            scores(n_full, 0)
            softmax(n_full - 1, 1)
            values(n_full - 2, 0)
            softmax(n_full, 0, masked=True)
            values(n_full - 1, 1)
            values(n_full, 0)

    for h in range(heads):
        hv = slice(h * V_HEAD, (h + 1) * V_HEAD)
        o_lat = (acc_ref[h] / lanes(l_ref[h], acc_ref.shape[2])).astype(_BF16)
        o_ref[:, hv] = _dot(o_lat, wuv_ref[:, hv]).astype(o_ref.dtype)


def _attn_prompt(qcat, kcat, ckvb, w_uv, *, batch, seq, tq, tk, heads):
    assert tk % tq == 0 and tq % CHUNK == 0 and seq % tk == 0
    h, t, e = qcat.shape
    c = ckvb.shape[1]
    nq = seq // tq
    q_block = lambda b, hg, i: (hg, b * nq + i, 0)
    stat = pltpu.VMEM((heads, tq, STAT_LANES), _F32)
    tile = pltpu.VMEM((heads, tq, tk), _F32)
    packed = pltpu.VMEM((heads, tq // 2, tk), jnp.uint32)
    return pl.pallas_call(
        functools.partial(_attn_prompt_kernel, tq=tq, tk=tk),
        grid=(batch, h // heads, nq),
        in_specs=[
            pl.BlockSpec((heads, tq, e), q_block),
            pl.BlockSpec((heads, seq, e), lambda b, hg, i: (hg, b, 0)),
            pl.BlockSpec((seq, c), lambda b, hg, i: (b, 0)),
            pl.BlockSpec((c, heads * V_HEAD), lambda b, hg, i: (0, hg)),
        ],
        out_specs=pl.BlockSpec((tq, heads * V_HEAD), lambda b, hg, i: (b * nq + i, hg)),
        out_shape=jax.ShapeDtypeStruct((t, h * V_HEAD), _BF16),
        scratch_shapes=[
            tile, tile,
            packed, packed,
            stat, stat,
            stat,
            stat,
            pltpu.VMEM((heads, tq, c), _F32),
        ],
        compiler_params=_params(("parallel", "parallel", "arbitrary")),
        name="attn_prompt",
    )(qcat, kcat, ckvb, w_uv)


def _attn_sample_kernel(ql_ref, qr_ref, pkv_ref, pkr_ref, kv_ref, kr_ref, wuv_ref, o_ref):
    pkv = pkv_ref[0].astype(_BF16)
    pkr = pkr_ref[0].astype(_BF16)
    kv = kv_ref[...]
    kr = kr_ref[...]
    hg = ql_ref.shape[0] // SAMPLE_HEAD_GROUPS
    rows = hg * ql_ref.shape[1]
    for g in range(SAMPLE_HEAD_GROUPS):
        hs = slice(g * hg, (g + 1) * hg)
        ql = ql_ref[hs].reshape(rows, ql_ref.shape[2])
        qr = qr_ref[hs].reshape(rows, qr_ref.shape[2])
        s_past = _dot_nt(ql, pkv) + _dot_nt(qr, pkr)
        s_new = _dot_nt(ql, kv) + _dot_nt(qr, kr)
        m = jnp.maximum(jnp.max(s_past, axis=1, keepdims=True),
                        jnp.max(s_new, axis=1, keepdims=True))
        p_past = jnp.exp2(s_past - m)
        p_new = jnp.exp2(s_new - m)
        l = jnp.sum(p_past, axis=1, keepdims=True) + jnp.sum(p_new, axis=1, keepdims=True)
        o = ((_dot(p_past.astype(_BF16), pkv) + _dot(p_new.astype(_BF16), kv)) / l).astype(_BF16)
        seq = ql_ref.shape[1]
        for hh in range(hg):
            hv = slice((g * hg + hh) * V_HEAD, (g * hg + hh + 1) * V_HEAD)
            o_ref[:, hv] = _dot(o[hh * seq:(hh + 1) * seq], wuv_ref[:, hv]).astype(o_ref.dtype)


def _attn_sample(qlat, qrope, past_kv, past_kr, ckvb, krb, w_uv, *, seq):
    h, t, c = qlat.shape
    r = qrope.shape[2]
    batch, past_len, _ = past_kv.shape
    head_row = lambda b: (0, b, 0)
    return pl.pallas_call(
        _attn_sample_kernel,
        grid=(batch,),
        in_specs=[
            pl.BlockSpec((h, seq, c), head_row),
            pl.BlockSpec((h, seq, r), head_row),
            pl.BlockSpec((1, past_len, c), lambda b: (b, 0, 0)),
            pl.BlockSpec((1, past_len, r), lambda b: (b, 0, 0)),
            pl.BlockSpec((seq, c), lambda b: (b, 0)),
            pl.BlockSpec((seq, r), lambda b: (b, 0)),
            pl.BlockSpec(w_uv.shape, lambda b: (0, 0)),
        ],
        out_specs=pl.BlockSpec((seq, h * V_HEAD), lambda b: (b, 0)),
        out_shape=jax.ShapeDtypeStruct((t, h * V_HEAD), _BF16),
        compiler_params=_params(("parallel",)),
        name="attn_sample",
    )(qlat, qrope, past_kv, past_kr, ckvb, krb, w_uv)


def _pool_delta(ext, pos0, group):
    n = ext.shape[0] - HALO
    pos = pos0 + lax.broadcasted_iota(jnp.int32, (n, 1), 0)
    out = []
    for gi, w in enumerate(POOL_WINDOWS):
        cols = ext[:, gi * group:(gi + 1) * group]
        s = cols
        shift = 1
        while shift < w:
            s = s + pltpu.roll(s, shift, axis=0)
            shift *= 2
        cnt = jnp.minimum(w, pos + 1).astype(_F32)
        out.append(s[HALO:] / cnt - cols[HALO:])
    return out


def _outproj_kernel(oattn_ref, u_ref, halo_ref, x_ref, wpool_ref, pscale_ref, wo_ref,
                    g_ref, b_ref, o_ref, *, alpha, seq, offset, blocks_per_seq):
    tm = u_ref.shape[0]
    group = u_ref.shape[1] // len(POOL_WINDOWS)
    u = u_ref[...]
    halo = halo_ref[...]
    if blocks_per_seq is not None:
        bi = pl.program_id(0) % blocks_per_seq
        halo = jnp.where(bi == 0, 0.0, halo)
        deltas = [_pool_delta(jnp.concatenate([halo, u], axis=0), offset + bi * tm, group)]
    else:
        deltas = []
        for s in range(tm // seq):
            ext = jnp.concatenate(
                [halo[s * HALO:(s + 1) * HALO], u[s * seq:(s + 1) * seq]], axis=0)
            deltas.append(_pool_delta(ext, offset, group))
    pooled = []
    for gi in range(len(POOL_WINDOWS)):
        d = jnp.concatenate([ds[gi] for ds in deltas], axis=0) if len(deltas) > 1 else deltas[0][gi]
        pooled.append(_dot(d.astype(_BF16), wpool_ref[gi]))
    o_pool = jnp.concatenate(pooled, axis=1) * pscale_ref[...]
    mixed = jnp.concatenate([oattn_ref[...], o_pool.astype(_BF16)], axis=1)
    for rs in _row_groups(tm):
        y = alpha * x_ref[rs, :] + _dot(mixed[rs], wo_ref[...])
        o_ref[rs, :] = _layer_norm(y, g_ref[...], b_ref[...])


def _outproj(oattn, u, halo_src, x, w_pool, pool_scale, w_o, g, b,
             *, alpha, seq, offset, tm, prompt):
    t, d = x.shape
    pw = u.shape[1]
    row = lambda i: (i, 0)
    const2 = lambda i: (0, 0)
    const3 = lambda i: (0, 0, 0)
    if prompt:
        blocks_per_seq = seq // tm
        step = tm // HALO
        halo_spec = pl.BlockSpec((HALO, pw), lambda i: (jnp.maximum(i * step - 1, 0), 0))
    else:
        blocks_per_seq = None
        halo_spec = pl.BlockSpec(((tm // seq) * HALO, pw), row)
    return pl.pallas_call(
        functools.partial(_outproj_kernel, alpha=alpha, seq=seq, offset=offset,
                          blocks_per_seq=blocks_per_seq),
        grid=(t // tm,),
        in_specs=[
            pl.BlockSpec((tm, oattn.shape[1]), row),
            pl.BlockSpec((tm, pw), row),
            halo_spec,
            pl.BlockSpec((tm, d), row),
            pl.BlockSpec(w_pool.shape, const3),
            pl.BlockSpec((1, pw), const2),
            pl.BlockSpec(w_o.shape, const2),
            pl.BlockSpec((1, d), const2),
            pl.BlockSpec((1, d), const2),
        ],
        out_specs=pl.BlockSpec((tm, d), row),
        out_shape=jax.ShapeDtypeStruct((t, d), _F32),
        compiler_params=_params(("parallel",)),
        name="outproj_prompt" if prompt else "outproj_sample",
    )(oattn, u, halo_src, x, w_pool, pool_scale, w_o, g, b)


def _ple_kernel(x_ref, p_ref, wg_ref, wp_ref, g_ref, b_ref, o_ref, *, alpha):
    for rs in _row_groups(x_ref.shape[0]):
        x = x_ref[rs, :]
        gate = jax.nn.sigmoid(_dot(x.astype(_BF16), wg_ref[...]))
        proj = _dot(p_ref[rs, :].astype(_BF16), wp_ref[...])
        o_ref[rs, :] = _layer_norm(alpha * x + gate * proj, g_ref[...], b_ref[...])


def _ple(x, p, w_gate, w_proj, g, b, *, alpha, tm):
    t, d = x.shape
    pd = p.shape[1]
    row = lambda i: (i, 0)
    const2 = lambda i: (0, 0)
    return pl.pallas_call(
        functools.partial(_ple_kernel, alpha=alpha),
        grid=(t // tm,),
        in_specs=[
            pl.BlockSpec((tm, d), row),
            pl.BlockSpec((tm, pd), row),
            pl.BlockSpec(w_gate.shape, const2),
            pl.BlockSpec(w_proj.shape, const2),
            pl.BlockSpec((1, d), const2),
            pl.BlockSpec((1, d), const2),
        ],
        out_specs=pl.BlockSpec((tm, d), row),
        out_shape=jax.ShapeDtypeStruct((t, d), _F32),
        compiler_params=_params(("parallel",)),
        name="ple",
    )(x, p, w_gate, w_proj, g, b)


def _rotate_half_cols(w):
    half = w.shape[-1] // 2
    return jnp.concatenate([-w[..., half:], w[..., :half]], axis=-1)


def _rope_table(pos, reps):
    inv = 1.0 / (ROPE_THETA ** (jnp.arange(0, QK_ROPE, 2, dtype=_F32) / QK_ROPE))
    ang = pos.astype(_F32)[:, None] * inv[None, :]
    ang = jnp.concatenate([ang, ang], -1)
    return jnp.tile(jnp.concatenate([jnp.cos(ang), jnp.sin(ang)], -1), (reps, 1))


def _tile_rows(total, preferred):
    tile = min(total, preferred)
    assert total % tile == 0
    return tile


def _prep_layer_weights(w):
    q_rank = w['g_q'].shape[0]
    kv_rank = w['g_kv'].shape[0]
    w_in = w['w_in']
    o_kr = q_rank + kv_rank
    w_kr = w_in[:, o_kr:o_kr + QK_ROPE]
    w_in_all = jnp.concatenate(
        [w_in[:, :o_kr], w_kr, _rotate_half_cols(w_kr), w_in[:, o_kr + QK_ROPE:]], axis=1)
    w_q = jnp.concatenate(
        [w['w_uq_nope'].reshape(q_rank, -1), w['w_uq_rope'].reshape(q_rank, -1),
         _rotate_half_cols(w['w_uq_rope']).reshape(q_rank, -1)], axis=1)
    row = lambda v: v.reshape(1, -1)
    return dict(
        ffn1_gu=w['ffn1_gu'].astype(_BF16), ffn1_down=w['ffn1_down'].astype(_BF16),
        ffn2_gu=w['ffn2_gu'].astype(_BF16), ffn2_down=w['ffn2_down'].astype(_BF16),
        w_in=w_in_all.astype(_BF16), w_q=w_q.astype(_BF16),
        w_uk=w['w_uk'].reshape(kv_rank, -1).astype(_BF16),
        w_uv=w['w_uv'].reshape(kv_rank, -1).astype(_BF16),
        w_pool=w['w_pool'].astype(_BF16), pool_scale=row(w['pool_scale']),
        w_o=w['w_o'].astype(_BF16),
        w_ple_gate=w['w_ple_gate'].astype(_BF16), w_ple_proj=w['w_ple_proj'].astype(_BF16),
        g_q=row(w['g_q']), g_kv=row(w['g_kv']),
        ln1=(row(w['ln1_g']), row(w['ln1_b'])), ln2=(row(w['ln2_g']), row(w['ln2_b'])),
        ln3=(row(w['ln3_g']), row(w['ln3_b'])), ln4=(row(w['ln4_g']), row(w['ln4_b'])),
    )


def _encoder_layer(x, p, past, pool_hist, offset, pw, *, alpha):
    bsz, seq, d = x.shape
    t = bsz * seq
    prompt = past is None
    xf = x.reshape(t, d)
    tm = _tile_rows(t, 512)
    d_ff = pw['ffn1_down'].shape[0]
    tf = 512 if d_ff % 512 == 0 else d_ff

    x1 = _ffn_ln(xf, pw['ffn1_gu'], pw['ffn1_down'], *pw['ln1'], alpha=alpha, tm=tm, tf=tf)

    tmi = _tile_rows(t, 256)
    if prompt:
        assert seq % tmi == 0
        tab = _rope_table(offset + jnp.arange(seq), 1)
    else:
        assert tmi % seq == 0
        tab = _rope_table(offset + jnp.arange(seq), tmi // seq)
    proj = _inproj(x1, pw['w_in'], pw['g_q'], pw['g_kv'], pw['w_q'], pw['w_uk'], tab, tm=tmi,
                   absorbed=not prompt)
    if prompt:
        qcat, kcat, ckv, kr, ckvb, u = proj
        tq = tk = min(seq, 512)
        oattn = _attn_prompt(qcat, kcat, ckvb, pw['w_uv'], batch=bsz, seq=seq, tq=tq, tk=tk, heads=2)
        halo_src = u
    else:
        qlat, qrope, ckv, kr, ckvb, krb, u = proj
        oattn = _attn_sample(qlat, qrope, past[0], past[1], ckvb, krb, pw['w_uv'], seq=seq)
        halo_src = jnp.pad(pool_hist, ((0, 0), (HALO - POOL_HIST, 0), (0, 0))).reshape(bsz * HALO, -1)

    tmo = _tile_rows(t, 512 if prompt else 256)
    if prompt:
        assert seq % tmo == 0 and tmo % HALO == 0
    else:
        assert tmo % seq == 0
    x2 = _outproj(oattn, u, halo_src, x1, pw['w_pool'], pw['pool_scale'], pw['w_o'],
                  *pw['ln2'], alpha=alpha, seq=seq, offset=offset, tm=tmo, prompt=prompt)
    x3 = _ffn_ln(x2, pw['ffn2_gu'], pw['ffn2_down'], *pw['ln3'], alpha=alpha, tm=tm, tf=tf)
    y = _ple(x3, p.reshape(t, -1), pw['w_ple_gate'], pw['w_ple_proj'], *pw['ln4'],
             alpha=alpha, tm=tm)

    u3 = u.reshape(bsz, seq, -1)
    if seq >= POOL_HIST:
        new_hist = u3[:, seq - POOL_HIST:]
    else:
        new_hist = jnp.concatenate([pool_hist, u3], axis=1)[:, -POOL_HIST:]
    return (y.reshape(bsz, seq, d), ckv.reshape(bsz, seq, -1), kr.reshape(bsz, seq, -1), new_hist)


def kernel(x_prompt, x_sample, cache_kv_latent, cache_k_rope, state_pool, p_prompt, p_sample, ffn1_gu, ffn1_down, ln1_g, ln1_b, w_in, g_q, g_kv, w_uq_nope, w_uq_rope, w_uk, w_uv, w_pool, pool_scale, w_o, ln2_g, ln2_b, ffn2_gu, ffn2_down, ln3_g, ln3_b, w_ple_gate, w_ple_proj, ln4_g, ln4_b):
    depth = ffn1_gu.shape[0]
    past_len = cache_kv_latent.shape[2]
    alpha = (2.0 * depth) ** 0.25
    stacked = dict(
        ffn1_gu=ffn1_gu, ffn1_down=ffn1_down, ln1_g=ln1_g, ln1_b=ln1_b, w_in=w_in, g_q=g_q,
        g_kv=g_kv, w_uq_nope=w_uq_nope, w_uq_rope=w_uq_rope, w_uk=w_uk, w_uv=w_uv,
        w_pool=w_pool, pool_scale=pool_scale, w_o=w_o, ln2_g=ln2_g, ln2_b=ln2_b,
        ffn2_gu=ffn2_gu, ffn2_down=ffn2_down, ln3_g=ln3_g, ln3_b=ln3_b,
        w_ple_gate=w_ple_gate, w_ple_proj=w_ple_proj, ln4_g=ln4_g, ln4_b=ln4_b)
    hp, hs = x_prompt, x_sample
    outs = [[] for _ in range(6)]
    for i in range(depth):
        pw = _prep_layer_weights({k: v[i] for k, v in stacked.items()})
        hp, c1, k1, s1 = _encoder_layer(hp, p_prompt[i], None, None, 0, pw, alpha=alpha)
        hs, c2, k2, s2 = _encoder_layer(
            hs, p_sample[i], (cache_kv_latent[i], cache_k_rope[i]), state_pool[i], past_len, pw,
            alpha=alpha)
        for acc, v in zip(outs, (c1, k1, s1, c2, k2, s2)):
            acc.append(v)
    return (hp, hs) + tuple(jnp.stack(v) for v in outs)
```
